```python
import math
import numpy as np
import jax
import jax.numpy as jnp
from jax import lax

D_MODEL = 1024
BATCH = 8
SEQ = 8192
DEPTH = 4

GRID_W = 64
CTX_LEN = 256
N_MIXERS = 3
FF_DIM = 4 * D_MODEL
NORM_EPS = 1e-6
GM_CHUNK = 128
GM_HALF = 3 * D_MODEL
GM_GROUPS = 8
NA_HEAD_DIM = 64
NA_HEADS = D_MODEL // NA_HEAD_DIM
NA_KH_MAX = 8
NA_KW = 16
NA_QBLOCK = 16
NA_KSPAN = 2 * NA_QBLOCK
ML_HEADS = 8
ML_V_DIM = D_MODEL // ML_HEADS
ML_QK_DIM = ML_V_DIM // 2
ML_INNER = ML_HEADS * ML_V_DIM
ML_CHUNK = 64
ROPE_BASE = 10000.0
N_A = (DEPTH + 2) // 3
N_B = (DEPTH + 1) // 3
N_C = DEPTH // 3

kernel_name = 'hybrid_gmlp_natten_mlstm_prefix_dit'


def rmsnorm(x, g):
    x32 = x.astype(jnp.float32)
    y = x32 * lax.rsqrt(jnp.mean(x32 * x32, axis=-1, keepdims=True) + NORM_EPS)
    return (y * g.astype(jnp.float32)).astype(x.dtype)


def layernorm(x, g):
    x32 = x.astype(jnp.float32)
    xc = x32 - jnp.mean(x32, axis=-1, keepdims=True)
    y = xc * lax.rsqrt(jnp.mean(xc * xc, axis=-1, keepdims=True) + NORM_EPS)
    return (y * g.astype(jnp.float32)).astype(x.dtype)


def sqrelu_mlp(h, w1, w2):
    return jnp.square(jax.nn.relu(h @ w1)) @ w2


def chunk_gmlp(h, w_in, b_in, ln_g, w_s, b_s, w_out):
    bsz, t, _ = h.shape
    z = jax.nn.gelu(h @ w_in + b_in)
    u, v = jnp.split(z, 2, axis=-1)
    v = layernorm(v, ln_g)
    v = v.reshape(bsz, t // GM_CHUNK, GM_CHUNK, GM_GROUPS, GM_HALF // GM_GROUPS)
    v = jnp.einsum('gpq,bnqgc->bnpgc', w_s, v) + b_s.T[None, None, :, :, None]
    return (u * v.reshape(bsz, t, GM_HALF)) @ w_out


def na_tables():
    cols = np.arange(GRID_W)
    win_start = np.clip(cols - NA_KW // 2, 0, GRID_W - NA_KW)
    n_blk = GRID_W // NA_QBLOCK
    blk_start = np.clip(win_start[::NA_QBLOCK], 0, GRID_W - NA_KSPAN)
    key_cols = blk_start[:, None] + np.arange(NA_KSPAN)[None, :]
    q_cols = cols.reshape(n_blk, NA_QBLOCK)
    ws = win_start.reshape(n_blk, NA_QBLOCK)[..., None]
    kc = key_cols[:, None, :]
    valid = (kc >= ws) & (kc < ws + NA_KW)
    dcol = np.clip(kc - q_cols[..., None], -(NA_KW - 1), NA_KW - 1) + NA_KW - 1
    return key_cols, valid, dcol


def neighbourhood_attention(h, hc, w_qkv, b_qkv, rpb, w_o, b_o, need_ctx):
    bsz, t, _ = h.shape
    lc = hc.shape[1]
    rows = t // GRID_W
    kh = min(NA_KH_MAX, rows)
    scale = NA_HEAD_DIM ** -0.5
    n_blk = GRID_W // NA_QBLOCK
    qkv = (h @ w_qkv + b_qkv).reshape(bsz, rows, GRID_W, 3, NA_HEADS, NA_HEAD_DIM)
    qkv = jnp.transpose(qkv, (3, 0, 4, 1, 2, 5))
    q, k, v = qkv[0], qkv[1], qkv[2]
    qkv_c = jnp.transpose((hc @ w_qkv + b_qkv).reshape(bsz, lc, 3, NA_HEADS, NA_HEAD_DIM), (2, 0, 3, 1, 4))
    qc, kc, vc = qkv_c[0], qkv_c[1], qkv_c[2]
    key_cols, valid, dcol = na_tables()
    mask = jnp.asarray(valid)[:, :, None, :]
    n_lat = kh * NA_KSPAN

    def row_step(r):
        rs = jnp.clip(r - kh // 2, 0, rows - kh)
        q_r = lax.dynamic_index_in_dim(q, r, axis=2, keepdims=False)
        q_r = q_r.reshape(bsz, NA_HEADS, n_blk, NA_QBLOCK, NA_HEAD_DIM)
        k_b = lax.dynamic_slice_in_dim(k, rs, kh, axis=2)[:, :, :, key_cols]
        v_b = lax.dynamic_slice_in_dim(v, rs, kh, axis=2)[:, :, :, key_cols]
        drow = rs + jnp.arange(kh) - r + NA_KH_MAX - 1
        bias = jnp.transpose(rpb[:, drow][:, :, dcol], (0, 2, 3, 1, 4))
        s_lat = jnp.einsum('bhjqd,bhrjkd->bhjqrk', q_r, k_b).astype(jnp.float32) * scale + bias.astype(jnp.float32)
        s_lat = jnp.where(mask, s_lat, jnp.float32(-1e30)).reshape(bsz, NA_HEADS, n_blk, NA_QBLOCK, n_lat)
        s_ctx = jnp.einsum('bhjqd,bhcd->bhjqc', q_r, kc).astype(jnp.float32) * scale
        p = jax.nn.softmax(jnp.concatenate([s_lat, s_ctx], axis=-1), axis=-1).astype(v.dtype)
        p_lat = p[..., :n_lat].reshape(bsz, NA_HEADS, n_blk, NA_QBLOCK, kh, NA_KSPAN)
        o = jnp.einsum('bhjqrk,bhrjkd->bhjqd', p_lat, v_b) + jnp.einsum('bhjqc,bhcd->bhjqd', p[..., n_lat:], vc)
        return o.reshape(bsz, NA_HEADS, GRID_W, NA_HEAD_DIM)

    out = lax.map(row_step, jnp.arange(rows))
    y = jnp.transpose(out, (1, 0, 3, 2, 4)).reshape(bsz, t, D_MODEL) @ w_o + b_o
    yc = None
    if need_ctx:
        sc = jnp.einsum('bhqd,bhkd->bhqk', qc, kc).astype(jnp.float32) * scale
        pc = jax.nn.softmax(sc, axis=-1).astype(vc.dtype)
        yc = jnp.einsum('bhqk,bhkd->bqhd', pc, vc).reshape(bsz, lc, D_MODEL) @ w_o + b_o
    return y, yc


def axial_rope(x):
    t = x.shape[1]
    pos = jnp.arange(t)
    row = (pos // GRID_W).astype(jnp.float32)
    col = (pos % GRID_W).astype(jnp.float32)
    d_axis = x.shape[-1] // 2
    inv = ROPE_BASE ** (-jnp.arange(0, d_axis, 2, dtype=jnp.float32) / d_axis)

    def rot(xa, p):
        ang = p[:, None] * inv[None, :]
        cos = jnp.cos(ang)[None, :, None, :]
        sin = jnp.sin(ang)[None, :, None, :]
        x1, x2 = jnp.split(xa, 2, axis=-1)
        return jnp.concatenate([x1 * cos - x2 * sin, x1 * sin + x2 * cos], axis=-1)

    x32 = x.astype(jnp.float32)
    return jnp.concatenate([rot(x32[..., :d_axis], row), rot(x32[..., d_axis:], col)], axis=-1).astype(x.dtype)


def mlstm_scan(q, k, v, ig, lf, state):
    bsz, nh, t, _ = q.shape
    dv = v.shape[-1]
    nc = t // ML_CHUNK
    tri = jnp.tril(jnp.ones((ML_CHUNK, ML_CHUNK), dtype=bool))

    def to_chunks(a):
        return jnp.moveaxis(a.reshape(bsz, nh, nc, ML_CHUNK, *a.shape[3:]), 2, 0)

    def step(carry, inp):
        c_st, n_st, m_st = carry
        qc, kc, vc, ic, fc = inp
        b = jnp.cumsum(fc, axis=-1)
        dmat = jnp.where(tri, b[..., :, None] - b[..., None, :] + ic[..., None, :], -jnp.inf)
        g = b + m_st[..., None]
        m_t = jnp.maximum(g, jnp.max(dmat, axis=-1))
        a = jnp.einsum('bhtd,bhsd->bhts', qc, kc) * jnp.exp(dmat - m_t[..., None])
        inter = jnp.exp(g - m_t)
        num = inter[..., None] * jnp.einsum('bhtd,bhde->bhte', qc, c_st) + jnp.einsum('bhts,bhse->bhte', a, vc)
        den = inter * jnp.einsum('bhtd,bhd->bht', qc, n_st) + jnp.sum(a, axis=-1)
        h = num / jnp.maximum(jnp.abs(den), jnp.exp(-m_t))[..., None]
        b_last = b[..., -1]
        a_s = b_last[..., None] - b + ic
        m_new = jnp.maximum(b_last + m_st, jnp.max(a_s, axis=-1))
        w_s = jnp.exp(a_s - m_new[..., None])
        decay = jnp.exp(b_last + m_st - m_new)
        c_new = decay[..., None, None] * c_st + jnp.einsum('bhs,bhsd,bhse->bhde', w_s, kc, vc)
        n_new = decay[..., None] * n_st + jnp.einsum('bhs,bhsd->bhd', w_s, kc)
        return (c_new, n_new, m_new), h

    state, h = lax.scan(step, state, (to_chunks(q), to_chunks(k), to_chunks(v), to_chunks(ig), to_chunks(lf)))
    return jnp.moveaxis(h, 0, 2).reshape(bsz, nh, t, dv), state


def mlstm_mixer(h, hc, w_in, w_gate, b_gate, norm_g, w_out, need_ctx):
    qk_w = ML_HEADS * ML_QK_DIM

    def heads(a):
        return jnp.transpose(a, (0, 2, 1, 3)).astype(jnp.float32)

    def project(z, rope):
        bsz, t, _ = z.shape
        p = z @ w_in
        q = p[..., :qk_w].reshape(bsz, t, ML_HEADS, ML_QK_DIM)
        k = p[..., qk_w:2 * qk_w].reshape(bsz, t, ML_HEADS, ML_QK_DIM)
        v = p[..., 2 * qk_w:2 * qk_w + ML_INNER].reshape(bsz, t, ML_HEADS, ML_V_DIM)
        o = p[..., 2 * qk_w + ML_INNER:]
        if rope:
            q, k = axial_rope(q), axial_rope(k)
        g = jnp.einsum('btd,zdg->zbgt', z, w_gate).astype(jnp.float32) + b_gate.astype(jnp.float32)[:, None, :, None]
        ig = g[:, :, :ML_HEADS]
        lf = jax.nn.log_sigmoid(g[:, :, ML_HEADS:])
        return heads(q), heads(k) * (ML_QK_DIM ** -0.5), heads(v), o, ig, lf

    def readout(hh, o):
        bsz, _, t, _ = hh.shape
        hh = hh * lax.rsqrt(jnp.mean(hh * hh, axis=-1, keepdims=True) + NORM_EPS)
        hh = jnp.transpose(hh, (0, 2, 1, 3)).reshape(bsz, t, ML_INNER) * norm_g.astype(jnp.float32)
        return (hh.astype(o.dtype) * jax.nn.sigmoid(o)) @ w_out

    def flip(a):
        return jnp.flip(a, axis=2)

    bsz = h.shape[0]
    q, k, v, o, ig, lf = project(h, True)
    qc, kc, vc, oc, igc, lfc = project(hc, False)
    zero = (jnp.zeros((bsz, ML_HEADS, ML_QK_DIM, ML_V_DIM), jnp.float32),
            jnp.zeros((bsz, ML_HEADS, ML_QK_DIM), jnp.float32),
            jnp.zeros((bsz, ML_HEADS), jnp.float32))
    hc_f, st_f = mlstm_scan(qc, kc, vc, igc[0], lfc[0], zero)
    h_f, _ = mlstm_scan(q, k, v, ig[0], lf[0], st_f)
    hc_b, st_b = mlstm_scan(flip(qc), flip(kc), flip(vc), flip(igc[1]), flip(lfc[1]), zero)
    h_b, _ = mlstm_scan(flip(q), flip(k), flip(v), flip(ig[1]), flip(lf[1]), st_b)
    y = readout(h_f + flip(h_b), o)
    yc = readout(hc_f + flip(hc_b), oc) if need_ctx else None
    return y, yc


def setup_inputs(seed: int = 0) -> dict:
    key = jax.random.key(seed)
    keys = jax.random.split(key, 28)

    def nrm(i, shape, scale):
        return jax.random.normal(keys[i], shape, jnp.float32) * scale

    D = D_MODEL
    qk_w = ML_HEADS * ML_QK_DIM
    forget_bias = jnp.linspace(3.0, 6.0, ML_HEADS, dtype=jnp.float32)[None, None, :]
    ml_b_gate = jnp.concatenate([nrm(22, (N_C, 2, ML_HEADS), 0.1),
                                 forget_bias + nrm(23, (N_C, 2, ML_HEADS), 0.1)], axis=-1)
    return {
        'x': nrm(0, (BATCH, SEQ, D), 1.0),
        'c': nrm(1, (BATCH, D), 1.0),
        'ctx': nrm(2, (BATCH, CTX_LEN, D), 1.0),
        'c_ctx': nrm(3, (D,), 1.0),
        'ada_w': nrm(4, (DEPTH, D, 6 * D), 0.5 * D ** -0.5),
        'ada_b': nrm(5, (DEPTH, 6 * D), 0.02),
        'norm_g': 1.0 + nrm(6, (DEPTH, 4, D), 0.05),
        'ffn_w1': nrm(7, (DEPTH, D, FF_DIM), D ** -0.5),
        'ffn_w2': nrm(8, (DEPTH, FF_DIM, D), FF_DIM ** -0.5),
        'gm_w_in': nrm(9, (N_A, D, 2 * GM_HALF), D ** -0.5),
        'gm_b_in': nrm(10, (N_A, 2 * GM_HALF), 0.02),
        'gm_ln_g': 1.0 + nrm(11, (N_A, GM_HALF), 0.05),
        'gm_ws': nrm(12, (N_A, GM_GROUPS, GM_CHUNK, GM_CHUNK), GM_CHUNK ** -0.5),
        'gm_bs': 1.0 + nrm(13, (N_A, GM_GROUPS, GM_CHUNK), 0.1),
        'gm_w_out': nrm(14, (N_A, GM_HALF, D), GM_HALF ** -0.5),
        'na_w_qkv': nrm(15, (N_B, D, 3 * D), D ** -0.5),
        'na_b_qkv': nrm(16, (N_B, 3 * D), 0.02),
        'na_rpb': nrm(17, (N_B, NA_HEADS, 2 * NA_KH_MAX - 1, 2 * NA_KW - 1), 0.1),
        'na_w_o': nrm(18, (N_B, D, D), D ** -0.5),
        'na_b_o': nrm(19, (N_B, D), 0.02),
        'ml_w_in': nrm(20, (N_C, D, 2 * qk_w + 2 * ML_INNER), D ** -0.5),
        'ml_w_gate': nrm(21, (N_C, 2, D, 2 * ML_HEADS), D ** -0.5),
        'ml_b_gate': ml_b_gate,
        'ml_norm_g': 1.0 + nrm(24, (N_C, ML_INNER), 0.05),
        'ml_w_out': nrm(25, (N_C, ML_INNER, D), ML_INNER ** -0.5),
    }


def reference(x, c, ctx, c_ctx, ada_w, ada_b, norm_g, ffn_w1, ffn_w2,
              gm_w_in, gm_b_in, gm_ln_g, gm_ws, gm_bs, gm_w_out,
              na_w_qkv, na_b_qkv, na_rpb, na_w_o, na_b_o,
              ml_w_in, ml_w_gate, ml_b_gate, ml_norm_g, ml_w_out):
    xc = ctx
    c_act = jax.nn.silu(c)
    cc_act = jax.nn.silu(c_ctx)
    for i in range(DEPTH):
        kind, j = i % N_MIXERS, i // N_MIXERS
        need_ctx = i < DEPTH - 1
        mod = [m[:, None, :] for m in jnp.split(c_act @ ada_w[i] + ada_b[i], 6, axis=-1)]
        modc = jnp.split(cc_act @ ada_w[i] + ada_b[i], 6, axis=-1)
        h = rmsnorm(x, norm_g[i, 0]) * (1.0 + mod[1]) + mod[0]
        hc = rmsnorm(xc, norm_g[i, 0]) * (1.0 + modc[1]) + modc[0]
        if kind == 0:
            y = chunk_gmlp(h, gm_w_in[j], gm_b_in[j], gm_ln_g[j], gm_ws[j], gm_bs[j], gm_w_out[j])
            yc = chunk_gmlp(hc, gm_w_in[j], gm_b_in[j], gm_ln_g[j], gm_ws[j], gm_bs[j], gm_w_out[j]) if need_ctx else None
        elif kind == 1:
            y, yc = neighbourhood_attention(h, hc, na_w_qkv[j], na_b_qkv[j], na_rpb[j], na_w_o[j], na_b_o[j], need_ctx)
        else:
            y, yc = mlstm_mixer(h, hc, ml_w_in[j], ml_w_gate[j], ml_b_gate[j], ml_norm_g[j], ml_w_out[j], need_ctx)
        x = x + mod[2] * rmsnorm(y, norm_g[i, 1])
        h = rmsnorm(x, norm_g[i, 2]) * (1.0 + mod[4]) + mod[3]
        x = x + mod[5] * rmsnorm(sqrelu_mlp(h, ffn_w1[i], ffn_w2[i]), norm_g[i, 3])
        if need_ctx:
            xc = xc + modc[2] * rmsnorm(yc, norm_g[i, 1])
            hc = rmsnorm(xc, norm_g[i, 2]) * (1.0 + modc[4]) + modc[3]
            xc = xc + modc[5] * rmsnorm(sqrelu_mlp(hc, ffn_w1[i], ffn_w2[i]), norm_g[i, 3])
    return x
```

```python
import functools

import numpy as np
import jax
import jax.numpy as jnp
from jax import lax
from jax.experimental import pallas as pl
from jax.experimental.pallas import tpu as pltpu

F32 = jnp.float32
BF16 = jnp.bfloat16

D = 1024
TB = 256
GRID_W = 64
EPS = 1e-6
CTX_ROW = 8
MOD_ROWS = 16
GM_HALF = 3 * D
GM_CHUNK = 128
GM_GROUPS = 8
GM_GW = GM_HALF // GM_GROUPS
NA_HEADS = 16
NA_DH = 64
NA_KH = 8
NA_KW = 16
NA_RB = TB // GRID_W
NA_KR = 3 * NA_RB
NEG = -1e30
ML_HEADS = 8
ML_DK = 64
ML_DV = 128
ROPE_BASE = 10000.0

VMEM_LIMIT = 56 * 1024 * 1024


def _cparams(sem):
    return pltpu.CompilerParams(dimension_semantics=sem, vmem_limit_bytes=VMEM_LIMIT)


def _rms(x, g):
    ms = jnp.mean(x * x, axis=-1, keepdims=True)
    return x * lax.rsqrt(ms + EPS) * g


def _premod(x, g, shift, scale):
    return _rms(x, g) * (1.0 + scale) + shift


def _const_spec(shape):
    nd = len(shape)
    return pl.BlockSpec(shape, lambda *_: (0,) * nd, pipeline_mode=pl.Buffered(1))


def _tok_spec(width, off=0, col=0):
    return pl.BlockSpec((None, TB, width), lambda b, j: (b, j + off, col))


def _mod_spec(ctx_first=True):
    if ctx_first:
        return pl.BlockSpec((None, 6, D), lambda b, j: (jnp.where(j == 0, CTX_ROW, b), 0, 0))
    return pl.BlockSpec((None, 6, D), lambda b, j: (b, 0, 0))


def _mod_kernel(c_ref, w_ref, b_ref, o_ref):
    c = c_ref[...]
    a = (c * jax.nn.sigmoid(c)).astype(BF16)
    o_ref[...] = jnp.dot(a, w_ref[...].astype(BF16), preferred_element_type=F32) + b_ref[...]


def _mods_all(c, c_ctx, ada_w, ada_b):
    depth = ada_w.shape[0]
    n = ada_w.shape[2]
    tn = 1536
    cv = jnp.zeros((MOD_ROWS, D), F32).at[:c.shape[0]].set(c).at[CTX_ROW].set(c_ctx)
    out = pl.pallas_call(
        _mod_kernel,
        grid=(depth, n // tn),
        in_specs=[pl.BlockSpec((MOD_ROWS, D), lambda l, k: (0, 0)),
                  pl.BlockSpec((None, D, tn), lambda l, k: (l, 0, k)),
                  pl.BlockSpec((None, 1, tn), lambda l, k: (l, 0, k))],
        out_specs=pl.BlockSpec((None, MOD_ROWS, tn), lambda l, k: (l, 0, k)),
        out_shape=jax.ShapeDtypeStruct((depth, MOD_ROWS, n), F32),
        compiler_params=_cparams(("arbitrary", "arbitrary")),
        name="adaln_mod",
    )(cv, ada_w, ada_b.reshape(depth, 1, n))
    return out.reshape(depth, MOD_ROWS, 6, D)


def _ffn_kernel(x_ref, mod_ref, g_ref, w1_ref, w2_ref, o_ref):
    x = x_ref[...]
    mod = mod_ref[...]
    g = g_ref[...]
    h = _premod(x, g[2:3], mod[3:4], mod[4:5]).astype(BF16)
    a = jnp.dot(h, w1_ref[...], preferred_element_type=F32)
    a = jnp.square(jnp.maximum(a, 0.0)).astype(BF16)
    y = jnp.dot(a, w2_ref[...], preferred_element_type=F32)
    o_ref[...] = x + mod[5:6] * _rms(y, g[3:4])


def _ffn_layer(xs, mods, ng, w1, w2, has_ctx, drop_ctx):
    bsz, ttot, _ = xs.shape
    off = 1 if drop_ctx else 0
    nblk = ttot // TB - off
    return pl.pallas_call(
        _ffn_kernel,
        grid=(bsz, nblk),
        in_specs=[_tok_spec(D, off), _mod_spec(has_ctx and not drop_ctx), _const_spec((4, D)),
                  _const_spec(w1.shape), _const_spec(w2.shape)],
        out_specs=_tok_spec(D),
        out_shape=jax.ShapeDtypeStruct((bsz, nblk * TB, D), F32),
        compiler_params=_cparams(("arbitrary", "arbitrary")),
        name="ffn",
    )(xs, mods, ng, w1, w2)


def _gelu_tanh(x):
    return 0.5 * x * (1.0 + jnp.tanh(np.float32(np.sqrt(2.0 / np.pi)) * (x + 0.044715 * (x * x * x))))


def _gmlp_kernel(x_ref, mod_ref, g_ref, win_ref, bin_ref, lng_ref, ws_ref, bst_ref, wout_ref,
                 o_ref, v_scr, t_scr):
    x = x_ref[...]
    mod = mod_ref[...]
    g = g_ref[...]
    h = _premod(x, g[0:1], mod[0:1], mod[1:2]).astype(BF16)
    v = _gelu_tanh(jnp.dot(h, win_ref[:, GM_HALF:], preferred_element_type=F32) + bin_ref[:, GM_HALF:])
    vc = v - jnp.mean(v, axis=-1, keepdims=True)
    v = vc * lax.rsqrt(jnp.mean(vc * vc, axis=-1, keepdims=True) + EPS) * lng_ref[...]
    v_scr[...] = v.astype(BF16)
    u = _gelu_tanh(jnp.dot(h, win_ref[:, :GM_HALF], preferred_element_type=F32) + bin_ref[:, :GM_HALF])
    bst = bst_ref[...]
    for n in range(TB // GM_CHUNK):
        rows = slice(n * GM_CHUNK, (n + 1) * GM_CHUNK)
        for gi in range(GM_GROUPS):
            cols = slice(gi * GM_GW, (gi + 1) * GM_GW)
            s = jnp.dot(ws_ref[gi], v_scr[rows, cols], preferred_element_type=F32) + bst[:, gi:gi + 1]
            t_scr[rows, cols] = (u[rows, cols] * s).astype(BF16)
    y = jnp.dot(t_scr[...], wout_ref[...], preferred_element_type=F32)
    o_ref[...] = x + mod[2:3] * _rms(y, g[1:2])


def _gmlp_layer(xs, mods, ng, w_in, b_in, ln_g, ws, bs, w_out, drop_ctx):
    bsz, ttot, _ = xs.shape
    off = 1 if drop_ctx else 0
    nblk = ttot // TB - off
    return pl.pallas_call(
        _gmlp_kernel,
        grid=(bsz, nblk),
        in_specs=[_tok_spec(D, off), _mod_spec(not drop_ctx), _const_spec((4, D)),
                  _const_spec(w_in.shape), _const_spec((1, 2 * GM_HALF)), _const_spec((1, GM_HALF)),
                  _const_spec(ws.shape), _const_spec((GM_CHUNK, GM_GROUPS)), _const_spec(w_out.shape)],
        out_specs=_tok_spec(D),
        out_shape=jax.ShapeDtypeStruct((bsz, nblk * TB, D), F32),
        scratch_shapes=[pltpu.VMEM((TB, GM_HALF), BF16), pltpu.VMEM((TB, GM_HALF), BF16)],
        compiler_params=_cparams(("arbitrary", "arbitrary")),
        name="gmlp",
    )(xs, mods, ng, w_in, b_in.reshape(1, -1), ln_g.reshape(1, -1), ws, bs.T, w_out)


def _inproj_kernel(x_ref, mod_ref, g_ref, w_ref, b_ref, o_ref):
    mod = mod_ref[...]
    g = g_ref[...]
    h = _premod(x_ref[...], g[0:1], mod[0:1], mod[1:2]).astype(BF16)
    o_ref[...] = (jnp.dot(h, w_ref[...], preferred_element_type=F32) + b_ref[...]).astype(BF16)


def _inproj_layer(xs, mods, ng, w, b):
    bsz, ttot, _ = xs.shape
    n = w.shape[1]
    return pl.pallas_call(
        _inproj_kernel,
        grid=(bsz, ttot // TB),
        in_specs=[_tok_spec(D), _mod_spec(), _const_spec((4, D)), _const_spec(w.shape), _const_spec((1, n))],
        out_specs=_tok_spec(n),
        out_shape=jax.ShapeDtypeStruct((bsz, ttot, n), BF16),
        compiler_params=_cparams(("arbitrary", "arbitrary")),
        name="na_inproj",
    )(xs, mods, ng, w, b.reshape(1, n))


def _outproj_kernel(x_ref, mod_ref, g_ref, y_ref, w_ref, b_ref, o_ref):
    mod = mod_ref[...]
    g = g_ref[...]
    y = jnp.dot(y_ref[...], w_ref[...], preferred_element_type=F32) + b_ref[...]
    o_ref[...] = x_ref[...] + mod[2:3] * _rms(y, g[1:2])


def _outproj_layer(xs, mods, ng, y, w, b):
    bsz, ttot, _ = xs.shape
    return pl.pallas_call(
        _outproj_kernel,
        grid=(bsz, ttot // TB),
        in_specs=[_tok_spec(D), _mod_spec(), _const_spec((4, D)), _tok_spec(D),
                  _const_spec(w.shape), _const_spec((1, D))],
        out_specs=_tok_spec(D),
        out_shape=jax.ShapeDtypeStruct((bsz, ttot, D), F32),
        compiler_params=_cparams(("arbitrary", "arbitrary")),
        name="na_outproj",
    )(xs, mods, ng, y, w, b.reshape(1, D))


def _na_bias_table(rpb, rows):
    assert rows >= 4 * NA_RB and rows % NA_RB == 0
    rr = np.arange(NA_RB)
    ib = np.arange(NA_KR)
    qc = np.arange(GRID_W)
    kc = np.arange(GRID_W)
    ws = np.clip(qc - NA_KW // 2, 0, GRID_W - NA_KW)
    valid_col = (kc[None, :] >= ws[:, None]) & (kc[None, :] < ws[:, None] + NA_KW)
    dcol = np.clip(kc[None, :] - qc[:, None], -(NA_KW - 1), NA_KW - 1) + NA_KW - 1
    tables = []
    for off, rs_minus_r in ((rr, np.full(NA_RB, -(NA_KH // 2))),
                            (np.full(NA_RB, NA_RB), -rr),
                            (np.zeros(NA_RB, np.int64), -(NA_KH // 2) - rr)):
        i = ib[None, :] - off[:, None]
        valid_row = (i >= 0) & (i < NA_KH)
        drow = np.clip(rs_minus_r[:, None] + i + NA_KH - 1, 0, 2 * NA_KH - 2)
        vals = rpb[:, drow[:, None, :, None], dcol[None, :, None, :]]
        valid = valid_row[:, None, :, None] & valid_col[None, :, None, :]
        vals = jnp.where(jnp.asarray(valid)[None], vals.astype(F32), NEG)
        tables.append(vals.reshape(NA_HEADS, TB, NA_KR * GRID_W))
    return jnp.stack(tables)


def _na_kernel(q_ref, kp_ref, kc_ref, kn_ref, kx_ref, vp_ref, vc_ref, vn_ref, vx_ref, bias_ref, o_ref):
    j = pl.program_id(2)
    lane = lax.broadcasted_iota(jnp.int32, (1, 2 * NA_DH), 1)
    dn = (((1,), (1,)), ((), ()))

    def heads(fn):
        q = q_ref[...]
        outs = []
        for a in range(2):
            sel = (lane >= a * NA_DH) & (lane < (a + 1) * NA_DH)
            qm = jnp.where(sel, q, jnp.zeros_like(q)) * jnp.asarray(NA_DH ** -0.5, BF16)
            outs.append(fn(a, qm))
        o_ref[...] = jnp.where(lane < NA_DH, outs[0], outs[1]).astype(BF16)

    @pl.when(j == 0)
    def _():
        def ctx_only(a, qm):
            s = lax.dot_general(qm, kx_ref[...], dn, preferred_element_type=F32)
            p = jnp.exp(s - jnp.max(s, axis=-1, keepdims=True))
            l = jnp.sum(p, axis=-1, keepdims=True)
            return jnp.dot(p.astype(BF16), vx_ref[...], preferred_element_type=F32) / l
        heads(ctx_only)

    @pl.when(j > 0)
    def _():
        def windowed(a, qm):
            krefs = (kp_ref, kc_ref, kn_ref)
            vrefs = (vp_ref, vc_ref, vn_ref, vx_ref)
            ss = [lax.dot_general(qm, krefs[i][...], dn, preferred_element_type=F32)
                  + bias_ref[a, :, i * TB:(i + 1) * TB] for i in range(3)]
            ss.append(lax.dot_general(qm, kx_ref[...], dn, preferred_element_type=F32))
            m = functools.reduce(jnp.maximum, [jnp.max(s, axis=-1, keepdims=True) for s in ss])
            ps = [jnp.exp(s - m) for s in ss]
            l = functools.reduce(jnp.add, [jnp.sum(p, axis=-1, keepdims=True) for p in ps])
            o = functools.reduce(jnp.add, [jnp.dot(p.astype(BF16), vr[...], preferred_element_type=F32)
                                           for p, vr in zip(ps, vrefs)])
            return o / l
        heads(windowed)


def _na_attention(qkv, bias):
    bsz, ttot, _ = qkv.shape
    nblk = ttot // TB
    hp = NA_HEADS // 2
    wb = 2 * NA_DH

    def tok(col0, fn):
        return pl.BlockSpec((None, TB, wb), lambda h, b, j: (b, fn(j), col0 + h))

    prev = lambda j: jnp.clip(j - 1, 1, nblk - 1)
    cur = lambda j: j
    nxt = lambda j: jnp.clip(j + 1, 1, nblk - 1)
    zero = lambda j: 0
    case = lambda j: jnp.where(j == 1, 1, jnp.where(j == nblk - 1, 2, 0))
    return pl.pallas_call(
        _na_kernel,
        grid=(hp, bsz, nblk),
        in_specs=[tok(0, cur),
                  tok(hp, prev), tok(hp, cur), tok(hp, nxt), tok(hp, zero),
                  tok(2 * hp, prev), tok(2 * hp, cur), tok(2 * hp, nxt), tok(2 * hp, zero),
                  pl.BlockSpec((None, 2, TB, NA_KR * GRID_W), lambda h, b, j: (case(j), h, 0, 0))],
        out_specs=tok(0, cur),
        out_shape=jax.ShapeDtypeStruct((bsz, ttot, D), BF16),
        compiler_params=_cparams(("arbitrary", "arbitrary", "arbitrary")),
        name="na_attention",
    )(qkv, qkv, qkv, qkv, qkv, qkv, qkv, qkv, qkv, bias)


def _na_layer(xs, mods, ng, w_qkv, b_qkv, rpb, w_o, b_o):
    rows = (xs.shape[1] - TB) // GRID_W
    qkv = _inproj_layer(xs, mods, ng, w_qkv, b_qkv)
    att = _na_attention(qkv, _na_bias_table(rpb, rows))
    return _outproj_layer(xs, mods, ng, att, w_o, b_o)


def _rope_tables(t_lat):
    pos = jnp.arange(t_lat)
    row = (pos // GRID_W).astype(F32)
    col = (pos % GRID_W).astype(F32)
    d_axis = ML_DK // 2
    inv = ROPE_BASE ** (-jnp.arange(0, d_axis, 2, dtype=F32) / d_axis)
    lane = np.arange(2 * ML_DK)
    d = lane % ML_DK
    use_col = (d // d_axis) == 1
    jdx = d % (d_axis // 2)
    first = (d % d_axis) < d_axis // 2
    ang_r = row[:, None] * inv[None, :]
    ang_c = col[:, None] * inv[None, :]
    ang = jnp.where(jnp.asarray(use_col)[None, :], ang_c[:, jdx], ang_r[:, jdx])
    cos = jnp.cos(ang)
    sin = jnp.sin(ang)
    ss = jnp.where(jnp.asarray(first)[None, :], -sin, sin)
    cos = jnp.concatenate([jnp.ones((TB, 2 * ML_DK), F32), cos], axis=0)
    ss = jnp.concatenate([jnp.zeros((TB, 2 * ML_DK), F32), ss], axis=0)
    return cos, ss


def _log_sigmoid(x):
    return jnp.minimum(x, 0.0) - jnp.log(1.0 + jnp.exp(-jnp.abs(x)))


def _ml_inproj_kernel(x_ref, mod_ref, g_ref, w_ref, cos_ref, ss_ref, wgc_ref, bgc_ref, wgr_ref, bgr_ref,
                      p_ref, gc_ref, gr_ref):
    mod = mod_ref[...]
    g = g_ref[...]
    h = _premod(x_ref[...], g[0:1], mod[0:1], mod[1:2]).astype(BF16)
    p = jnp.dot(h, w_ref[...], preferred_element_type=F32)
    cos = cos_ref[...]
    ss = ss_ref[...]
    lane = lax.broadcasted_iota(jnp.int32, (1, 2 * ML_DK), 1)
    first = (lane % (ML_DK // 2)) < ML_DK // 4
    qk_w = ML_HEADS * ML_DK
    for blk in range(2 * qk_w // 128):
        cols = slice(blk * 128, (blk + 1) * 128)
        xb = p[:, cols]
        sw = jnp.where(first, pltpu.roll(xb, 128 - ML_DK // 4, 1), pltpu.roll(xb, ML_DK // 4, 1))
        y = xb * cos + sw * ss
        if blk * 128 >= qk_w:
            y = y * (ML_DK ** -0.5)
        p_ref[:, cols] = y.astype(BF16)
    p_ref[:, 2 * qk_w:] = p[:, 2 * qk_w:].astype(BF16)

    r_i = lax.broadcasted_iota(jnp.int32, (TB, TB), 0)
    c_i = lax.broadcasted_iota(jnp.int32, (TB, TB), 1)
    low = (c_i <= r_i).astype(F32)
    upp = (c_i >= r_i).astype(F32)
    ng = 4 * ML_HEADS
    hp = lax.Precision.HIGHEST

    gc = jnp.dot(h, wgc_ref[...], preferred_element_type=F32) + bgc_ref[...]
    idx = lax.broadcasted_iota(jnp.int32, (1, ng), 1)
    gc = jnp.where((idx // ML_HEADS) % 2 == 1, _log_sigmoid(gc), gc)
    cl = jnp.dot(low, gc, preferred_element_type=F32, precision=hp)
    cu = jnp.dot(upp, gc, preferred_element_type=F32, precision=hp)
    gc_ref[...] = jnp.where(idx // ML_HEADS == 1, cl, jnp.where(idx // ML_HEADS == 3, cu, gc))

    gr = lax.dot_general(wgr_ref[...], h, (((1,), (1,)), ((), ())), preferred_element_type=F32) + bgr_ref[...]
    idr = lax.broadcasted_iota(jnp.int32, (ng, 1), 0)
    gr = jnp.where((idr // ML_HEADS) % 2 == 1, _log_sigmoid(gr), gr)
    rf = jnp.dot(gr, upp, preferred_element_type=F32, precision=hp)
    rb = jnp.dot(gr, low, preferred_element_type=F32, precision=hp)
    gr_ref[...] = jnp.where(idr // ML_HEADS == 1, rf, jnp.where(idr // ML_HEADS == 3, rb, gr))


def _ml_inproj(xs, mods, ng, w_in, w_gate, b_gate):
    bsz, ttot, _ = xs.shape
    n = w_in.shape[1]
    ngate = 4 * ML_HEADS
    cos, ss = _rope_tables(ttot - TB)
    wgc = jnp.concatenate([w_gate[0], w_gate[1]], axis=1).astype(BF16)
    bg = b_gate.reshape(ngate).astype(F32)
    tab_spec = pl.BlockSpec((TB, 2 * ML_DK), lambda b, j: (j, 0))
    return pl.pallas_call(
        _ml_inproj_kernel,
        grid=(bsz, ttot // TB),
        in_specs=[_tok_spec(D), _mod_spec(), _const_spec((4, D)), _const_spec(w_in.shape),
                  tab_spec, tab_spec,
                  _const_spec((D, ngate)), _const_spec((1, ngate)),
                  _const_spec((ngate, D)), _const_spec((ngate, 1))],
        out_specs=[_tok_spec(n), _tok_spec(ngate),
                   pl.BlockSpec((None, ngate, TB), lambda b, j: (b, 0, j))],
        out_shape=[jax.ShapeDtypeStruct((bsz, ttot, n), BF16),
                   jax.ShapeDtypeStruct((bsz, ttot, ngate), F32),
                   jax.ShapeDtypeStruct((bsz, ngate, ttot), F32)],
        compiler_params=_cparams(("arbitrary", "arbitrary")),
        name="ml_inproj",
    )(xs, mods, ng, w_in, cos, ss, wgc, bg.reshape(1, ngate), wgc.T, bg.reshape(ngate, 1))


def _ml_scan_kernel(qkf_ref, vf_ref, gcf_ref, grf_ref, qkb_ref, vb_ref, gcb_ref, grb_ref,
                    hf_ref, hb_ref, c_scr, m_scr):
    step = pl.program_id(1)

    @pl.when(step == 0)
    def _():
        c_scr[...] = jnp.zeros_like(c_scr)
        m_scr[...] = jnp.zeros_like(m_scr)

    qk_w = ML_HEADS * ML_DK
    lane = lax.broadcasted_iota(jnp.int32, (1, 2 * ML_DK), 1)
    r_i = lax.broadcasted_iota(jnp.int32, (TB, TB), 0)
    c_i = lax.broadcasted_iota(jnp.int32, (TB, TB), 1)
    ones_col = (lax.broadcasted_iota(jnp.int32, (TB, ML_DV), 1) == 0).astype(BF16)
    dn_t = (((1,), (1,)), ((), ()))
    dn_0 = (((0,), (0,)), ((), ()))

    dirs = ((qkf_ref, vf_ref, gcf_ref, grf_ref, hf_ref, c_i <= r_i, TB - 1),
            (qkb_ref, vb_ref, gcb_ref, grb_ref, hb_ref, c_i >= r_i, 0))
    for dirn, (qk_ref, v_ref, gc_ref, gr_ref, out_ref, tri, last) in enumerate(dirs):
        base = dirn * 2 * ML_HEADS
        for hd in range(ML_HEADS):
            sidx = dirn * ML_HEADS + hd
            pair, a = divmod(hd, 2)
            q = qk_ref[:, pair * 128:(pair + 1) * 128]
            k = qk_ref[:, qk_w + pair * 128:qk_w + (pair + 1) * 128]
            sel = (lane >= a * ML_DK) & (lane < (a + 1) * ML_DK)
            km = jnp.where(sel, k, jnp.zeros_like(k))
            vaug = jnp.concatenate([v_ref[:, hd * ML_DV:(hd + 1) * ML_DV], ones_col], axis=1)
            ig_col = gc_ref[:, base + hd:base + hd + 1]
            b_col = gc_ref[:, base + ML_HEADS + hd:base + ML_HEADS + hd + 1]
            ig_row = gr_ref[base + hd:base + hd + 1, :]
            b_row = gr_ref[base + ML_HEADS + hd:base + ML_HEADS + hd + 1, :]
            m_st = m_scr[sidx][:, 0:1]
            c_st = c_scr[sidx]

            dmat = jnp.where(tri, b_col - (b_row - ig_row), -jnp.inf)
            gg = b_col + m_st
            m_t = jnp.maximum(gg, jnp.max(dmat, axis=-1, keepdims=True))
            s = lax.dot_general(q, km, dn_t, preferred_element_type=F32)
            amat = (s * jnp.exp(dmat - m_t)).astype(BF16)
            res = (jnp.dot(amat, vaug, preferred_element_type=F32)
                   + jnp.exp(gg - m_t) * jnp.dot(q, c_st.astype(BF16), preferred_element_type=F32))
            den = jnp.maximum(jnp.abs(res[:, ML_DV:ML_DV + 1]), jnp.exp(-m_t))
            out_ref[:, hd * ML_DV:(hd + 1) * ML_DV] = res[:, :ML_DV] * (1.0 / den)

            b_last = b_row[:, last:last + 1]
            a_s = b_last - b_col + ig_col
            m_new = jnp.maximum(b_last + m_st, jnp.max(a_s, axis=0, keepdims=True))
            kw = (km.astype(F32) * jnp.exp(a_s - m_new)).astype(BF16)
            c_scr[sidx] = (jnp.exp(b_last + m_st - m_new) * c_st
                           + lax.dot_general(kw, vaug, dn_0, preferred_element_type=F32))
            m_scr[sidx] = jnp.broadcast_to(m_new, (1, 128))


def _ml_scan(p, gcol, grow):
    bsz, ttot, _ = p.shape
    nblk = ttot // TB
    ngate = 4 * ML_HEADS
    fwd = lambda i: i
    bwd = lambda i: jnp.where(i == 0, 0, nblk - i)

    def specs(fn):
        return [pl.BlockSpec((None, TB, 2 * ML_HEADS * ML_DK), lambda b, i: (b, fn(i), 0)),
                pl.BlockSpec((None, TB, ML_HEADS * ML_DV), lambda b, i: (b, fn(i), 1)),
                pl.BlockSpec((None, TB, ngate), lambda b, i: (b, fn(i), 0)),
                pl.BlockSpec((None, ngate, TB), lambda b, i: (b, 0, fn(i)))]

    hw = ML_HEADS * ML_DV
    return pl.pallas_call(
        _ml_scan_kernel,
        grid=(bsz, nblk),
        in_specs=specs(fwd) + specs(bwd),
        out_specs=[pl.BlockSpec((None, TB, hw), lambda b, i: (b, fwd(i), 0)),
                   pl.BlockSpec((None, TB, hw), lambda b, i: (b, bwd(i), 0))],
        out_shape=[jax.ShapeDtypeStruct((bsz, ttot, hw), F32)] * 2,
        scratch_shapes=[pltpu.VMEM((2 * ML_HEADS, 2 * ML_DK, 2 * ML_DV), F32),
                        pltpu.VMEM((2 * ML_HEADS, 1, 128), F32)],
        compiler_params=_cparams(("arbitrary", "arbitrary")),
        name="ml_scan",
    )(p, p, gcol, grow, p, p, gcol, grow)


def _ml_readout_kernel(x_ref, mod_ref, g_ref, hf_ref, hb_ref, og_ref, mlg_ref, w_ref, o_ref):
    mod = mod_ref[...]
    g = g_ref[...]
    hs = hf_ref[...] + hb_ref[...]
    parts = []
    for hd in range(ML_HEADS):
        xh = hs[:, hd * ML_DV:(hd + 1) * ML_DV]
        parts.append(xh * lax.rsqrt(jnp.mean(xh * xh, axis=-1, keepdims=True) + EPS))
    hh = jnp.concatenate(parts, axis=1) * mlg_ref[...]
    yin = (hh * jax.nn.sigmoid(og_ref[...].astype(F32))).astype(BF16)
    y = jnp.dot(yin, w_ref[...], preferred_element_type=F32)
    o_ref[...] = x_ref[...] + mod[2:3] * _rms(y, g[1:2])


def _ml_readout(xs, mods, ng, hf, hb, p, ml_g, w_out):
    bsz, ttot, _ = xs.shape
    hw = ML_HEADS * ML_DV
    return pl.pallas_call(
        _ml_readout_kernel,
        grid=(bsz, ttot // TB),
        in_specs=[_tok_spec(D), _mod_spec(), _const_spec((4, D)), _tok_spec(hw), _tok_spec(hw),
                  _tok_spec(hw, col=2), _const_spec((1, hw)), _const_spec(w_out.shape)],
        out_specs=_tok_spec(D),
        out_shape=jax.ShapeDtypeStruct((bsz, ttot, D), F32),
        compiler_params=_cparams(("arbitrary", "arbitrary")),
        name="ml_readout",
    )(xs, mods, ng, hf, hb, p, ml_g.reshape(1, hw), w_out)


def _ml_layer(xs, mods, ng, w_in, w_gate, b_gate, ml_g, w_out):
    p, gcol, grow = _ml_inproj(xs, mods, ng, w_in, w_gate, b_gate)
    hf, hb = _ml_scan(p, gcol, grow)
    return _ml_readout(xs, mods, ng, hf, hb, p, ml_g, w_out)


def kernel(x, c, ctx, c_ctx, ada_w, ada_b, norm_g, ffn_w1, ffn_w2, gm_w_in, gm_b_in, gm_ln_g, gm_ws, gm_bs, gm_w_out, na_w_qkv, na_b_qkv, na_rpb, na_w_o, na_b_o, ml_w_in, ml_w_gate, ml_b_gate, ml_norm_g, ml_w_out):
    depth = ada_w.shape[0]
    assert ctx.shape[1] == TB and x.shape[1] % TB == 0 and x.shape[0] <= CTX_ROW
    mods_all = _mods_all(c, c_ctx, ada_w, ada_b)
    xs = jnp.concatenate([ctx, x], axis=1)
    for i in range(depth):
        kind, j = i % 3, i // 3
        mods = mods_all[i]
        ng = norm_g[i]
        last = i == depth - 1
        if kind == 0:
            xs = _gmlp_layer(xs, mods, ng, gm_w_in[j].astype(BF16), gm_b_in[j], gm_ln_g[j],
                             gm_ws[j].astype(BF16), gm_bs[j], gm_w_out[j].astype(BF16), last)
        elif kind == 1:
            xs = _na_layer(xs, mods, ng, na_w_qkv[j].astype(BF16), na_b_qkv[j], na_rpb[j],
                           na_w_o[j].astype(BF16), na_b_o[j])
        else:
            xs = _ml_layer(xs, mods, ng, ml_w_in[j].astype(BF16), ml_w_gate[j], ml_b_gate[j],
                           ml_norm_g[j], ml_w_out[j].astype(BF16))
        has_ctx = xs.shape[1] != x.shape[1]
        xs = _ffn_layer(xs, mods, ng, ffn_w1[i].astype(BF16), ffn_w2[i].astype(BF16), has_ctx, has_ctx and last)
    return xs
```

```python
import functools

import numpy as np
import jax
import jax.numpy as jnp
from jax import lax
from jax.experimental import pallas as pl
from jax.experimental.pallas import tpu as pltpu

F32 = jnp.float32
BF16 = jnp.bfloat16

D = 1024
TB = 256
GRID_W = 64
EPS = 1e-6
CTX_ROW = 8
MOD_ROWS = 16
GM_HALF = 3 * D
GM_CHUNK = 128
GM_GROUPS = 8
GM_GW = GM_HALF // GM_GROUPS
NA_HEADS = 16
NA_DH = 64
NA_KH = 8
NA_KW = 16
NA_RB = TB // GRID_W
NA_KR = 3 * NA_RB
NEG = -1e30
LOG2E = float(np.log2(np.e))
NA_QSCALE = NA_DH ** -0.5 * LOG2E
NA_PAIRS = 4
ML_HEADS = 8
ML_DK = 64
ML_DV = 128
ML_QK = ML_HEADS * ML_DK
ML_INNER = ML_HEADS * ML_DV
ML_ONES = 16
ROPE_BASE = 10000.0

VMEM_LIMIT = 56 * 1024 * 1024

DN_T = (((1,), (1,)), ((), ()))
DN_0 = (((0,), (0,)), ((), ()))


def _cparams(sem, flags=None):
    return pltpu.CompilerParams(dimension_semantics=sem, vmem_limit_bytes=VMEM_LIMIT, flags=flags)


def _rms(x, g):
    ms = jnp.mean(x * x, axis=-1, keepdims=True)
    return x * lax.rsqrt(ms + EPS) * g


def _premod(x, g, shift, scale):
    return _rms(x, g) * (1.0 + scale) + shift


def _const_spec(shape):
    nd = len(shape)
    return pl.BlockSpec(shape, lambda *_: (0,) * nd, pipeline_mode=pl.Buffered(1))


def _tok_spec(width, off=0, col=0):
    return pl.BlockSpec((None, TB, width), lambda b, j: (b, j + off, col))


def _tokT_spec(height, row=0):
    return pl.BlockSpec((None, height, TB), lambda b, j: (b, row, j))


def _mod_spec(ctx_first=True):
    if ctx_first:
        return pl.BlockSpec((None, 6, D), lambda b, j: (jnp.where(j == 0, CTX_ROW, b), 0, 0))
    return pl.BlockSpec((None, 6, D), lambda b, j: (b, 0, 0))


def _mod_kernel(c_ref, w_ref, b_ref, o_ref):
    c = c_ref[...]
    a = (c * jax.nn.sigmoid(c)).astype(BF16)
    o_ref[...] = jnp.dot(a, w_ref[...].astype(BF16), preferred_element_type=F32) + b_ref[...]


def _mods_all(c, c_ctx, ada_w, ada_b):
    depth = ada_w.shape[0]
    n = ada_w.shape[2]
    tn = 1536
    cv = jnp.zeros((MOD_ROWS, D), F32).at[:c.shape[0]].set(c).at[CTX_ROW].set(c_ctx)
    out = pl.pallas_call(
        _mod_kernel,
        grid=(depth, n // tn),
        in_specs=[pl.BlockSpec((MOD_ROWS, D), lambda l, k: (0, 0)),
                  pl.BlockSpec((None, D, tn), lambda l, k: (l, 0, k)),
                  pl.BlockSpec((None, 1, tn), lambda l, k: (l, 0, k))],
        out_specs=pl.BlockSpec((None, MOD_ROWS, tn), lambda l, k: (l, 0, k)),
        out_shape=jax.ShapeDtypeStruct((depth, MOD_ROWS, n), F32),
        compiler_params=_cparams(("arbitrary", "arbitrary")),
        name="adaln_mod",
    )(cv, ada_w, ada_b.reshape(depth, 1, n))
    return out.reshape(depth, MOD_ROWS, 6, D)


def _ffn_kernel(x_ref, mod_ref, g_ref, w1_ref, w2_ref, o_ref):
    x = x_ref[...]
    mod = mod_ref[...]
    g = g_ref[...]
    h = _premod(x, g[2:3], mod[3:4], mod[4:5]).astype(BF16)
    a = jnp.dot(h, w1_ref[...], preferred_element_type=F32)
    a = jnp.square(jnp.maximum(a, 0.0)).astype(BF16)
    y = jnp.dot(a, w2_ref[...], preferred_element_type=F32)
    o_ref[...] = x + mod[5:6] * _rms(y, g[3:4])


def _ffn_layer(xs, mods, ng, w1, w2, has_ctx, drop_ctx):
    bsz, ttot, _ = xs.shape
    off = 1 if drop_ctx else 0
    nblk = ttot // TB - off
    return pl.pallas_call(
        _ffn_kernel,
        grid=(bsz, nblk),
        in_specs=[_tok_spec(D, off), _mod_spec(has_ctx and not drop_ctx), _const_spec((4, D)),
                  _const_spec(w1.shape), _const_spec(w2.shape)],
        out_specs=_tok_spec(D),
        out_shape=jax.ShapeDtypeStruct((bsz, nblk * TB, D), F32),
        compiler_params=_cparams(("arbitrary", "arbitrary")),
        name="ffn",
    )(xs, mods, ng, w1, w2)


def _gelu_tanh(x):
    return 0.5 * x * (1.0 + jnp.tanh(np.float32(np.sqrt(2.0 / np.pi)) * (x + 0.044715 * (x * x * x))))


def _gmlp_kernel(x_ref, mod_ref, g_ref, win_ref, bin_ref, lng_ref, ws_ref, bst_ref, wout_ref,
                 o_ref, v_scr, t_scr):
    x = x_ref[...]
    mod = mod_ref[...]
    g = g_ref[...]
    h = _premod(x, g[0:1], mod[0:1], mod[1:2]).astype(BF16)
    v = _gelu_tanh(jnp.dot(h, win_ref[:, GM_HALF:], preferred_element_type=F32) + bin_ref[:, GM_HALF:])
    vc = v - jnp.mean(v, axis=-1, keepdims=True)
    v = vc * lax.rsqrt(jnp.mean(vc * vc, axis=-1, keepdims=True) + EPS) * lng_ref[...]
    v_scr[...] = v.astype(BF16)
    u = _gelu_tanh(jnp.dot(h, win_ref[:, :GM_HALF], preferred_element_type=F32) + bin_ref[:, :GM_HALF])
    bst = bst_ref[...]
    for n in range(TB // GM_CHUNK):
        rows = slice(n * GM_CHUNK, (n + 1) * GM_CHUNK)
        for gi in range(GM_GROUPS):
            cols = slice(gi * GM_GW, (gi + 1) * GM_GW)
            s = jnp.dot(ws_ref[gi], v_scr[rows, cols], preferred_element_type=F32) + bst[:, gi:gi + 1]
            t_scr[rows, cols] = (u[rows, cols] * s).astype(BF16)
    y = jnp.dot(t_scr[...], wout_ref[...], preferred_element_type=F32)
    o_ref[...] = x + mod[2:3] * _rms(y, g[1:2])


def _gmlp_layer(xs, mods, ng, w_in, b_in, ln_g, ws, bs, w_out, drop_ctx):
    bsz, ttot, _ = xs.shape
    off = 1 if drop_ctx else 0
    nblk = ttot // TB - off
    return pl.pallas_call(
        _gmlp_kernel,
        grid=(bsz, nblk),
        in_specs=[_tok_spec(D, off), _mod_spec(not drop_ctx), _const_spec((4, D)),
                  _const_spec(w_in.shape), _const_spec((1, 2 * GM_HALF)), _const_spec((1, GM_HALF)),
                  _const_spec(ws.shape), _const_spec((GM_CHUNK, GM_GROUPS)), _const_spec(w_out.shape)],
        out_specs=_tok_spec(D),
        out_shape=jax.ShapeDtypeStruct((bsz, nblk * TB, D), F32),
        scratch_shapes=[pltpu.VMEM((TB, GM_HALF), BF16), pltpu.VMEM((TB, GM_HALF), BF16)],
        compiler_params=_cparams(("arbitrary", "arbitrary")),
        name="gmlp",
    )(xs, mods, ng, w_in, b_in.reshape(1, -1), ln_g.reshape(1, -1), ws, bs.T, w_out)


def _na_inproj_kernel(x_ref, mod_ref, g_ref, wk_ref, bk_ref, wt_ref, bt_ref, k_ref, pt_ref):
    mod = mod_ref[...]
    g = g_ref[...]
    h = _premod(x_ref[...], g[0:1], mod[0:1], mod[1:2]).astype(BF16)
    k_ref[...] = (jnp.dot(h, wk_ref[...], preferred_element_type=F32) + bk_ref[...]).astype(BF16)
    bt = bt_ref[...]
    pt = lax.dot_general(wt_ref[...], h, DN_T, preferred_element_type=F32) + jnp.concatenate([bt, bt], axis=1)
    pt_ref[:D, :] = (pt[:D] * NA_QSCALE).astype(BF16)
    pt_ref[D:, :] = pt[D:].astype(BF16)


def _na_inproj(xs, mods, ng, w_qkv, b_qkv):
    bsz, ttot, _ = xs.shape
    wk = w_qkv[:, D:2 * D]
    wt = jnp.concatenate([w_qkv[:, :D], w_qkv[:, 2 * D:]], axis=1).T
    bt = jnp.concatenate([b_qkv[:D], b_qkv[2 * D:]]).astype(F32)
    bt = jnp.broadcast_to(bt[:, None], (2 * D, 128))
    return pl.pallas_call(
        _na_inproj_kernel,
        grid=(bsz, ttot // TB),
        in_specs=[_tok_spec(D), _mod_spec(), _const_spec((4, D)), _const_spec((D, D)), _const_spec((1, D)),
                  _const_spec((2 * D, D)), _const_spec((2 * D, 128))],
        out_specs=[_tok_spec(D), _tokT_spec(2 * D)],
        out_shape=[jax.ShapeDtypeStruct((bsz, ttot, D), BF16),
                   jax.ShapeDtypeStruct((bsz, 2 * D, ttot), BF16)],
        compiler_params=_cparams(("arbitrary", "arbitrary")),
        name="na_inproj",
    )(xs, mods, ng, wk, b_qkv[D:2 * D].reshape(1, D), wt, bt)


def _na_bias_table(rpb, rows):
    assert rows >= 4 * NA_RB and rows % NA_RB == 0
    rr = np.arange(NA_RB)
    ib = np.arange(NA_KR)
    qc = np.arange(GRID_W)
    kc = np.arange(GRID_W)
    ws = np.clip(qc - NA_KW // 2, 0, GRID_W - NA_KW)
    valid_col = (kc[None, :] >= ws[:, None]) & (kc[None, :] < ws[:, None] + NA_KW)
    dcol = np.clip(kc[None, :] - qc[:, None], -(NA_KW - 1), NA_KW - 1) + NA_KW - 1
    tiles = jnp.where(jnp.asarray(valid_col)[None, None], rpb[:, :, dcol].astype(F32) * LOG2E, NEG)
    drows, valids = [], []
    for off, rs_minus_r in ((rr, np.full(NA_RB, -(NA_KH // 2))),
                            (np.full(NA_RB, NA_RB), -rr),
                            (np.zeros(NA_RB, np.int64), -(NA_KH // 2) - rr)):
        i = ib[None, :] - off[:, None]
        valids.append((i >= 0) & (i < NA_KH))
        drows.append(np.clip(rs_minus_r[:, None] + i + NA_KH - 1, 0, 2 * NA_KH - 2))
    drow = np.stack(drows)
    valid_row = np.stack(valids)
    t = tiles[:, drow]
    t = jnp.where(jnp.asarray(valid_row)[None, :, :, :, None, None], t, NEG)
    t = t.reshape(NA_HEADS // 2, 2, 3, NA_RB, NA_KR, GRID_W, GRID_W)
    t = jnp.transpose(t, (2, 0, 4, 6, 1, 3, 5))
    return t.reshape(3, NA_HEADS // 2, NA_KR * GRID_W, 2 * TB)


def _na_kernel(q_ref, kp_ref, kc_ref, kn_ref, kx_ref, vp_ref, vc_ref, vn_ref, vx_ref, bias_ref, o_ref):
    j = pl.program_id(2)
    row = lax.broadcasted_iota(jnp.int32, (2 * NA_DH, 1), 0)

    def heads(krefs, vrefs, nbias):
        for pair in range(NA_PAIRS):
            blk = slice(pair * 2 * NA_DH, (pair + 1) * 2 * NA_DH)
            q = q_ref[blk, :]
            zq = jnp.zeros_like(q)
            q2 = jnp.concatenate([jnp.where(row < NA_DH, q, zq), jnp.where(row < NA_DH, zq, q)], axis=1)
            kcat = jnp.concatenate([kr[:, blk] for kr in krefs], axis=0)
            s = jnp.dot(kcat, q2, preferred_element_type=F32)
            ss = [s[i * TB:(i + 1) * TB] + bias_ref[pair, i * TB:(i + 1) * TB, :] if i < nbias
                  else s[i * TB:(i + 1) * TB] for i in range(len(krefs))]
            m = functools.reduce(jnp.maximum, [jnp.max(s, axis=0, keepdims=True) for s in ss])
            ps = [jnp.exp2(s - m) for s in ss]
            l = functools.reduce(jnp.add, [jnp.sum(p, axis=0, keepdims=True) for p in ps])
            vcat = jnp.concatenate([vr[blk, :] for vr in vrefs], axis=1)
            pcat = jnp.concatenate([p.astype(BF16) for p in ps], axis=0)
            o = jnp.dot(vcat, pcat, preferred_element_type=F32) * (1.0 / l)
            o_ref[blk, :] = jnp.where(row < NA_DH, o[:, :TB], o[:, TB:]).astype(BF16)

    @pl.when(j == 0)
    def _():
        heads((kx_ref,), (vx_ref,), 0)

    @pl.when(j > 0)
    def _():
        heads((kp_ref, kc_ref, kn_ref, kx_ref), (vp_ref, vc_ref, vn_ref, vx_ref), 3)


def _na_attention(kt, pt, bias):
    bsz, ttot, _ = kt.shape
    nblk = ttot // TB
    hp = NA_HEADS // 2 // NA_PAIRS
    wb = 2 * NA_DH * NA_PAIRS

    def tok(fn):
        return pl.BlockSpec((None, TB, wb), lambda h, b, j: (b, fn(j), h))

    def tokT(row0, fn):
        return pl.BlockSpec((None, wb, TB), lambda h, b, j: (b, row0 + h, fn(j)))

    prev = lambda j: jnp.clip(j - 1, 1, nblk - 1)
    cur = lambda j: j
    nxt = lambda j: jnp.clip(j + 1, 1, nblk - 1)
    zero = lambda j: 0
    case = lambda j: jnp.where(j == 1, 1, jnp.where(j == nblk - 1, 2, 0))
    return pl.pallas_call(
        _na_kernel,
        grid=(hp, bsz, nblk),
        in_specs=[tokT(0, cur),
                  tok(prev), tok(cur), tok(nxt), tok(zero),
                  tokT(hp, prev), tokT(hp, cur), tokT(hp, nxt), tokT(hp, zero),
                  pl.BlockSpec((None, NA_PAIRS, NA_KR * GRID_W, 2 * TB), lambda h, b, j: (case(j), h, 0, 0))],
        out_specs=tokT(0, cur),
        out_shape=jax.ShapeDtypeStruct((bsz, D, ttot), BF16),
        compiler_params=_cparams(("arbitrary", "arbitrary", "arbitrary")),
        name="na_attention",
    )(pt, kt, kt, kt, kt, pt, pt, pt, pt, bias)


def _na_outproj_kernel(x_ref, mod_ref, g_ref, yt_ref, w_ref, b_ref, o_ref):
    mod = mod_ref[...]
    g = g_ref[...]
    y = lax.dot_general(yt_ref[...], w_ref[...], DN_0, preferred_element_type=F32) + b_ref[...]
    o_ref[...] = x_ref[...] + mod[2:3] * _rms(y, g[1:2])


def _na_outproj(xs, mods, ng, yt, w, b):
    bsz, ttot, _ = xs.shape
    return pl.pallas_call(
        _na_outproj_kernel,
        grid=(bsz, ttot // TB),
        in_specs=[_tok_spec(D), _mod_spec(), _const_spec((4, D)), _tokT_spec(D),
                  _const_spec(w.shape), _const_spec((1, D))],
        out_specs=_tok_spec(D),
        out_shape=jax.ShapeDtypeStruct((bsz, ttot, D), F32),
        compiler_params=_cparams(("arbitrary", "arbitrary")),
        name="na_outproj",
    )(xs, mods, ng, yt, w, b.reshape(1, D))


def _na_layer(xs, mods, ng, w_qkv, b_qkv, rpb, w_o, b_o):
    rows = (xs.shape[1] - TB) // GRID_W
    kt, pt = _na_inproj(xs, mods, ng, w_qkv, b_qkv)
    att = _na_attention(kt, pt, _na_bias_table(rpb, rows))
    return _na_outproj(xs, mods, ng, att, w_o, b_o)


def _rope_tables(t_lat):
    pos = jnp.arange(t_lat)
    row = (pos // GRID_W).astype(F32)
    col = (pos % GRID_W).astype(F32)
    d_axis = ML_DK // 2
    inv = ROPE_BASE ** (-jnp.arange(0, d_axis, 2, dtype=F32) / d_axis)
    lane = np.arange(2 * ML_DK)
    d = lane % ML_DK
    use_col = (d // d_axis) == 1
    jdx = d % (d_axis // 2)
    first = (d % d_axis) < d_axis // 2
    ang_r = row[:, None] * inv[None, :]
    ang_c = col[:, None] * inv[None, :]
    ang = jnp.where(jnp.asarray(use_col)[None, :], ang_c[:, jdx], ang_r[:, jdx])
    cos = jnp.cos(ang)
    sin = jnp.sin(ang)
    ss = jnp.where(jnp.asarray(first)[None, :], -sin, sin)
    cos = jnp.concatenate([jnp.ones((TB, 2 * ML_DK), F32), cos], axis=0)
    ss = jnp.concatenate([jnp.zeros((TB, 2 * ML_DK), F32), ss], axis=0)
    return cos, ss


def _log_sigmoid(x):
    return jnp.minimum(x, 0.0) - jnp.log(1.0 + jnp.exp(-jnp.abs(x)))


def _lane_scan(x, op, ident, reverse):
    n = x.shape[1]
    lane = lax.broadcasted_iota(jnp.int32, (1, n), 1)
    d = 1
    while d < n:
        if reverse:
            sh = jnp.where(lane < n - d, pltpu.roll(x, n - d, 1), ident)
        else:
            sh = jnp.where(lane >= d, pltpu.roll(x, d, 1), ident)
        x = op(x, sh)
        d *= 2
    return x


def _ml_inproj_kernel(x_ref, mod_ref, g_ref, wk_ref, wt_ref, cos_ref, ss_ref, cost_ref, sst_ref,
                      wg_ref, bg_ref, k_ref, pt_ref, gr_ref, gc_ref):
    mod = mod_ref[...]
    g = g_ref[...]
    h = _premod(x_ref[...], g[0:1], mod[0:1], mod[1:2]).astype(BF16)
    half = ML_DK // 4

    k = jnp.dot(h, wk_ref[...], preferred_element_type=F32)
    cos = cos_ref[...]
    ss = ss_ref[...]
    lane = lax.broadcasted_iota(jnp.int32, (1, 2 * ML_DK), 1)
    first_l = (lane % (2 * half)) < half
    for blk in range(ML_QK // 128):
        cols = slice(blk * 128, (blk + 1) * 128)
        xb = k[:, cols]
        sw = jnp.where(first_l, pltpu.roll(xb, 128 - half, 1), pltpu.roll(xb, half, 1))
        k_ref[:, cols] = ((xb * cos + sw * ss) * (ML_DK ** -0.5)).astype(BF16)

    pt = lax.dot_general(wt_ref[...], h, DN_T, preferred_element_type=F32)
    pt_ref[:2 * ML_INNER, :] = pt[:2 * ML_INNER].astype(BF16)
    cost = cost_ref[...]
    sst = sst_ref[...]
    row = lax.broadcasted_iota(jnp.int32, (2 * ML_DK, 1), 0)
    first_r = (row % (2 * half)) < half
    for blk in range(ML_QK // 128):
        rows = slice(2 * ML_INNER + blk * 128, 2 * ML_INNER + (blk + 1) * 128)
        xb = pt[rows]
        sw = jnp.where(first_r, pltpu.roll(xb, 128 - half, 0), pltpu.roll(xb, half, 0))
        pt_ref[rows, :] = (xb * cost + sw * sst).astype(BF16)

    nh = ML_HEADS
    bg = bg_ref[...]
    gr = lax.dot_general(wg_ref[...], h, DN_T, preferred_element_type=F32) + jnp.concatenate([bg, bg], axis=1)
    outs = []
    for dirn in range(2):
        ig = gr[2 * dirn * nh:(2 * dirn + 1) * nh]
        lf = _log_sigmoid(gr[(2 * dirn + 1) * nh:(2 * dirn + 2) * nh])
        b = _lane_scan(lf, jnp.add, 0.0, dirn == 1)
        u = ig - b
        outs += [u, _lane_scan(u, jnp.maximum, -jnp.inf, dirn == 1), b]
    gr_ref[...] = jnp.concatenate(outs, axis=0)
    ucols = jnp.concatenate([outs[0], outs[3], jnp.zeros((128 - 2 * nh, TB), F32)], axis=0)
    gc_ref[...] = ucols.T


def _ml_inproj(xs, mods, ng, w_in, w_gate, b_gate):
    bsz, ttot, _ = xs.shape
    ngate = 4 * ML_HEADS
    cos, ss = _rope_tables(ttot - TB)
    wk = w_in[:, ML_QK:2 * ML_QK]
    wt = jnp.concatenate([w_in[:, 2 * ML_QK + ML_INNER:], w_in[:, 2 * ML_QK:2 * ML_QK + ML_INNER],
                          w_in[:, :ML_QK]], axis=1).T
    wg = jnp.concatenate([w_gate[0], w_gate[1]], axis=1).astype(BF16).T
    bg = jnp.broadcast_to(b_gate.reshape(ngate, 1).astype(F32), (ngate, 128))
    nt = wt.shape[0]
    tab = pl.BlockSpec((TB, 2 * ML_DK), lambda b, j: (j, 0))
    tabT = pl.BlockSpec((2 * ML_DK, TB), lambda b, j: (0, j))
    return pl.pallas_call(
        _ml_inproj_kernel,
        grid=(bsz, ttot // TB),
        in_specs=[_tok_spec(D), _mod_spec(), _const_spec((4, D)), _const_spec(wk.shape), _const_spec(wt.shape),
                  tab, tab, tabT, tabT, _const_spec((ngate, D)), _const_spec((ngate, 128))],
        out_specs=[_tok_spec(ML_QK), _tokT_spec(nt), _tokT_spec(6 * ML_HEADS), _tok_spec(128)],
        out_shape=[jax.ShapeDtypeStruct((bsz, ttot, ML_QK), BF16),
                   jax.ShapeDtypeStruct((bsz, nt, ttot), BF16),
                   jax.ShapeDtypeStruct((bsz, 6 * ML_HEADS, ttot), F32),
                   jax.ShapeDtypeStruct((bsz, ttot, 128), F32)],
        compiler_params=_cparams(("arbitrary", "arbitrary")),
        name="ml_inproj",
    )(xs, mods, ng, wk, wt, cos, ss, cos.T, ss.T, wg, bg)


def _ml_scan_kernel(kf_ref, vf_ref, qf_ref, grf_ref, gcf_ref, kb_ref, vb_ref, qb_ref, grb_ref, gcb_ref,
                    hf_ref, hb_ref, s_scr, m_scr):
    step = pl.program_id(1)

    @pl.when(step == 0)
    def _():
        s_scr[...] = jnp.zeros_like(s_scr)
        m_scr[...] = jnp.zeros_like(m_scr)

    nh = ML_HEADS
    lane = lax.broadcasted_iota(jnp.int32, (1, 2 * ML_DK), 1)
    r_i = lax.broadcasted_iota(jnp.int32, (TB, TB), 0)
    c_i = lax.broadcasted_iota(jnp.int32, (TB, TB), 1)
    ones_rows = jnp.ones((ML_ONES, TB), BF16)

    dirs = ((kf_ref, vf_ref, qf_ref, grf_ref, gcf_ref, hf_ref, r_i <= c_i, TB - 1),
            (kb_ref, vb_ref, qb_ref, grb_ref, gcb_ref, hb_ref, r_i >= c_i, 0))
    for dirn, (k_ref, v_ref, q_ref, gr_ref, gc_ref, out_ref, tri, last) in enumerate(dirs):
        u = gr_ref[0:nh, :]
        pm = gr_ref[nh:2 * nh, :]
        b = gr_ref[2 * nh:3 * nh, :]
        m_st = m_scr[dirn][:, 0:1]
        mrow = jnp.maximum(m_st, pm)
        inter = jnp.exp(m_st - mrow)
        em = jnp.exp(-(b + mrow))
        m_last = mrow[:, last:last + 1]
        wrow = jnp.exp(u - m_last)
        decay = jnp.exp(m_st - m_last)
        m_scr[dirn] = jnp.broadcast_to(b[:, last:last + 1] + m_last, (nh, 128))
        for hd in range(nh):
            sidx = dirn * nh + hd
            pair, a = divmod(hd, 2)
            kp = k_ref[:, pair * 128:(pair + 1) * 128]
            sel = (lane >= a * ML_DK) & (lane < (a + 1) * ML_DK)
            km = jnp.where(sel, kp, jnp.zeros_like(kp))
            qt = q_ref[pair * 128:(pair + 1) * 128, :]
            vaug = jnp.concatenate([v_ref[hd * ML_DV:(hd + 1) * ML_DV, :], ones_rows], axis=0)
            st = s_scr[sidx]
            s_t = jnp.dot(km, qt, preferred_element_type=F32)
            u_col = gc_ref[:, sidx:sidx + 1]
            d_t = jnp.where(tri, jnp.exp(u_col - mrow[hd:hd + 1, :]), 0.0)
            a_t = (s_t * d_t).astype(BF16)
            res = (jnp.dot(vaug, a_t, preferred_element_type=F32)
                   + inter[hd:hd + 1, :] * jnp.dot(st.astype(BF16), qt, preferred_element_type=F32))
            den = jnp.maximum(jnp.abs(res[ML_DV:ML_DV + 1, :]), em[hd:hd + 1, :])
            out_ref[hd * ML_DV:(hd + 1) * ML_DV, :] = res[:ML_DV] * (1.0 / den)
            vw = (vaug.astype(F32) * wrow[hd:hd + 1, :]).astype(BF16)
            s_scr[sidx] = decay[hd:hd + 1, :] * st + jnp.dot(vw, km, preferred_element_type=F32)


def _ml_scan(kt, pt, grow, gcol):
    bsz, ttot, _ = kt.shape
    nblk = ttot // TB
    fwd = lambda i: i
    bwd = lambda i: jnp.where(i == 0, 0, nblk - i)
    vrow = ML_INNER // ML_INNER
    qrow = 2 * ML_INNER // ML_QK

    def specs(fn, dirn):
        return [pl.BlockSpec((None, TB, ML_QK), lambda b, i: (b, fn(i), 0)),
                pl.BlockSpec((None, ML_INNER, TB), lambda b, i: (b, vrow, fn(i))),
                pl.BlockSpec((None, ML_QK, TB), lambda b, i: (b, qrow, fn(i))),
                pl.BlockSpec((None, 3 * ML_HEADS, TB), lambda b, i: (b, dirn, fn(i))),
                pl.BlockSpec((None, TB, 128), lambda b, i: (b, fn(i), 0))]

    return pl.pallas_call(
        _ml_scan_kernel,
        grid=(bsz, nblk),
        in_specs=specs(fwd, 0) + specs(bwd, 1),
        out_specs=[pl.BlockSpec((None, ML_INNER, TB), lambda b, i: (b, 0, fwd(i))),
                   pl.BlockSpec((None, ML_INNER, TB), lambda b, i: (b, 0, bwd(i)))],
        out_shape=[jax.ShapeDtypeStruct((bsz, ML_INNER, ttot), F32)] * 2,
        scratch_shapes=[pltpu.VMEM((2 * ML_HEADS, ML_DV + ML_ONES, 2 * ML_DK), F32),
                        pltpu.VMEM((2, ML_HEADS, 128), F32)],
        compiler_params=_cparams(("arbitrary", "arbitrary")),
        name="ml_scan",
    )(kt, pt, pt, grow, gcol, kt, pt, pt, grow, gcol)


def _ml_readout_kernel(x_ref, mod_ref, g_ref, hf_ref, hb_ref, og_ref, mlg_ref, w_ref, o_ref):
    mod = mod_ref[...]
    g = g_ref[...]
    hs = hf_ref[...] + hb_ref[...]
    parts = []
    for hd in range(ML_HEADS):
        xh = hs[hd * ML_DV:(hd + 1) * ML_DV]
        parts.append(xh * lax.rsqrt(jnp.mean(xh * xh, axis=0, keepdims=True) + EPS))
    mlg = mlg_ref[...]
    hh = jnp.concatenate(parts, axis=0) * jnp.concatenate([mlg, mlg], axis=1)
    yin = (hh * jax.nn.sigmoid(og_ref[...].astype(F32))).astype(BF16)
    y = lax.dot_general(yin, w_ref[...], DN_0, preferred_element_type=F32)
    o_ref[...] = x_ref[...] + mod[2:3] * _rms(y, g[1:2])


def _ml_readout(xs, mods, ng, hf, hb, pt, ml_g, w_out):
    bsz, ttot, _ = xs.shape
    mlg = jnp.broadcast_to(ml_g.astype(F32)[:, None], (ML_INNER, 128))
    return pl.pallas_call(
        _ml_readout_kernel,
        grid=(bsz, ttot // TB),
        in_specs=[_tok_spec(D), _mod_spec(), _const_spec((4, D)), _tokT_spec(ML_INNER), _tokT_spec(ML_INNER),
                  _tokT_spec(ML_INNER), _const_spec((ML_INNER, 128)), _const_spec(w_out.shape)],
        out_specs=_tok_spec(D),
        out_shape=jax.ShapeDtypeStruct((bsz, ttot, D), F32),
        compiler_params=_cparams(("arbitrary", "arbitrary")),
        name="ml_readout",
    )(xs, mods, ng, hf, hb, pt, mlg, w_out)


def _ml_layer(xs, mods, ng, w_in, w_gate, b_gate, ml_g, w_out):
    kt, pt, grow, gcol = _ml_inproj(xs, mods, ng, w_in, w_gate, b_gate)
    hf, hb = _ml_scan(kt, pt, grow, gcol)
    return _ml_readout(xs, mods, ng, hf, hb, pt, ml_g, w_out)


def kernel(x, c, ctx, c_ctx, ada_w, ada_b, norm_g, ffn_w1, ffn_w2, gm_w_in, gm_b_in, gm_ln_g, gm_ws, gm_bs, gm_w_out, na_w_qkv, na_b_qkv, na_rpb, na_w_o, na_b_o, ml_w_in, ml_w_gate, ml_b_gate, ml_norm_g, ml_w_out):
    depth = ada_w.shape[0]
    assert ctx.shape[1] == TB and x.shape[1] % TB == 0 and x.shape[0] <= CTX_ROW
    mods_all = _mods_all(c, c_ctx, ada_w, ada_b)
    xs = jnp.concatenate([ctx, x], axis=1)
    for i in range(depth):
        kind, j = i % 3, i // 3
        mods = mods_all[i]
        ng = norm_g[i]
        last = i == depth - 1
        if kind == 0:
            xs = _gmlp_layer(xs, mods, ng, gm_w_in[j].astype(BF16), gm_b_in[j], gm_ln_g[j],
                             gm_ws[j].astype(BF16), gm_bs[j], gm_w_out[j].astype(BF16), last)
        elif kind == 1:
            xs = _na_layer(xs, mods, ng, na_w_qkv[j].astype(BF16), na_b_qkv[j], na_rpb[j],
                           na_w_o[j].astype(BF16), na_b_o[j])
        else:
            xs = _ml_layer(xs, mods, ng, ml_w_in[j].astype(BF16), ml_w_gate[j], ml_b_gate[j],
                           ml_norm_g[j], ml_w_out[j].astype(BF16))
        has_ctx = xs.shape[1] != x.shape[1]
        xs = _ffn_layer(xs, mods, ng, ffn_w1[i].astype(BF16), ffn_w2[i].astype(BF16), has_ctx, has_ctx and last)
    return xs
```

```python
import functools

import numpy as np
import jax
import jax.numpy as jnp
from jax import lax
from jax.experimental import pallas as pl
from jax.experimental.pallas import tpu as pltpu

F32 = jnp.float32
BF16 = jnp.bfloat16

D = 1024
TB = 256
GRID_W = 64
EPS = 1e-6
CTX_ROW = 8
MOD_ROWS = 16
GM_HALF = 3 * D
GM_CHUNK = 128
GM_GROUPS = 8
GM_GW = GM_HALF // GM_GROUPS
NA_HEADS = 16
NA_DH = 64
NA_KH = 8
NA_KW = 16
NA_RB = TB // GRID_W
NA_KR = 3 * NA_RB
NEG = -1e30
LOG2E = float(np.log2(np.e))
NA_QSCALE = NA_DH ** -0.5 * LOG2E
NA_PAIRS = 4
ML_HEADS = 8
ML_DK = 64
ML_DV = 128
ML_QK = ML_HEADS * ML_DK
ML_INNER = ML_HEADS * ML_DV
ML_ONES = 16
ROPE_BASE = 10000.0

VMEM_LIMIT = 56 * 1024 * 1024

DN_T = (((1,), (1,)), ((), ()))
DN_0 = (((0,), (0,)), ((), ()))


def _cparams(sem, flags=None):
    return pltpu.CompilerParams(dimension_semantics=sem, vmem_limit_bytes=VMEM_LIMIT, flags=flags)


def _rms(x, g):
    ms = jnp.mean(x * x, axis=-1, keepdims=True)
    return x * lax.rsqrt(ms + EPS) * g


def _premod(x, g, shift, scale):
    return _rms(x, g) * (1.0 + scale) + shift


def _const_spec(shape):
    nd = len(shape)
    return pl.BlockSpec(shape, lambda *_: (0,) * nd, pipeline_mode=pl.Buffered(1))


def _tok_spec(width, off=0, col=0):
    return pl.BlockSpec((None, TB, width), lambda b, j: (b, j + off, col))


def _tokT_spec(height, row=0):
    return pl.BlockSpec((None, height, TB), lambda b, j: (b, row, j))


def _mod_spec(ctx_first=True):
    if ctx_first:
        return pl.BlockSpec((None, 6, D), lambda b, j: (jnp.where(j == 0, CTX_ROW, b), 0, 0))
    return pl.BlockSpec((None, 6, D), lambda b, j: (b, 0, 0))


def _mod_kernel(c_ref, w_ref, b_ref, o_ref):
    c = c_ref[...]
    a = (c * jax.nn.sigmoid(c)).astype(BF16)
    o_ref[...] = jnp.dot(a, w_ref[...].astype(BF16), preferred_element_type=F32) + b_ref[...]


def _mods_all(c, c_ctx, ada_w, ada_b):
    depth = ada_w.shape[0]
    n = ada_w.shape[2]
    tn = 1536
    cv = jnp.zeros((MOD_ROWS, D), F32).at[:c.shape[0]].set(c).at[CTX_ROW].set(c_ctx)
    out = pl.pallas_call(
        _mod_kernel,
        grid=(depth, n // tn),
        in_specs=[pl.BlockSpec((MOD_ROWS, D), lambda l, k: (0, 0)),
                  pl.BlockSpec((None, D, tn), lambda l, k: (l, 0, k)),
                  pl.BlockSpec((None, 1, tn), lambda l, k: (l, 0, k))],
        out_specs=pl.BlockSpec((None, MOD_ROWS, tn), lambda l, k: (l, 0, k)),
        out_shape=jax.ShapeDtypeStruct((depth, MOD_ROWS, n), F32),
        compiler_params=_cparams(("arbitrary", "arbitrary")),
        name="adaln_mod",
    )(cv, ada_w, ada_b.reshape(depth, 1, n))
    return out.reshape(depth, MOD_ROWS, 6, D)


def _software_pipeline(n_items, stages):
    carry = [None] * n_items
    for t in range(n_items + len(stages) - 1):
        for k, stage in enumerate(stages):
            if 0 <= t - k < n_items:
                carry[t - k] = stage(t - k, carry[t - k])


def _sub_blocks(nblk):
    return 3 if nblk % 3 == 0 else (2 if nblk % 2 == 0 else 1)


def _ffn_kernel(*refs, n_sub, mixer):
    per = {None: 2, "na": 3, "ml": 5}[mixer]
    subs = [refs[s * per:(s + 1) * per] for s in range(n_sub)]
    consts, o_ref = refs[n_sub * per:-1], refs[-1]
    g = consts[0][...]
    w1_ref, w2_ref = consts[-2:]

    def mixer_proj(s, _):
        if mixer == "na":
            wo_ref, bo_ref = consts[1:3]
            return lax.dot_general(subs[s][2][...], wo_ref[...], DN_0, preferred_element_type=F32) + bo_ref[...]
        hf_ref, hb_ref, og_ref = subs[s][2:5]
        mlg_ref, wo_ref = consts[1:3]
        hs = hf_ref[...] + hb_ref[...]
        parts = []
        for hd in range(ML_HEADS):
            xh = hs[hd * ML_DV:(hd + 1) * ML_DV]
            parts.append(xh * lax.rsqrt(jnp.mean(xh * xh, axis=0, keepdims=True) + EPS))
        mlg = mlg_ref[...]
        hh = jnp.concatenate(parts, axis=0) * jnp.concatenate([mlg, mlg], axis=1)
        yin = (hh * jax.nn.sigmoid(og_ref[...].astype(F32))).astype(BF16)
        return lax.dot_general(yin, wo_ref[...], DN_0, preferred_element_type=F32)

    def up(s, ymix):
        x = subs[s][0][...]
        mod = subs[s][1][...]
        if mixer is not None:
            x = x + mod[2:3] * _rms(ymix, g[1:2])
        h = _premod(x, g[2:3], mod[3:4], mod[4:5]).astype(BF16)
        return x, jnp.dot(h, w1_ref[...], preferred_element_type=F32)

    def down(s, c):
        a = jnp.square(jnp.maximum(c[1], 0.0)).astype(BF16)
        return c[0], jnp.dot(a, w2_ref[...], preferred_element_type=F32)

    def finish(s, c):
        mod = subs[s][1][...]
        o_ref[s * TB:(s + 1) * TB, :] = c[0] + mod[5:6] * _rms(c[1], g[3:4])

    _software_pipeline(n_sub, ([] if mixer is None else [mixer_proj]) + [up, down, finish])


def _ffn_layer(xs, mods, ng, w1, w2, has_ctx, drop_ctx, mixer=None):
    bsz, ttot, _ = xs.shape
    off = 1 if drop_ctx else 0
    nblk = ttot // TB - off
    n_sub = 1 if (mixer is not None and mixer[0] == "ml") else _sub_blocks(nblk)
    ctx_first = has_ctx and not drop_ctx

    def blk(s):
        return lambda j: off + n_sub * j + s

    def mod_spec(s):
        row = (lambda b, j: jnp.where(blk(s)(j) == 0, CTX_ROW, b)) if ctx_first else (lambda b, j: b)
        return pl.BlockSpec((None, 6, D), lambda b, j: (row(b, j), 0, 0))

    def tok(s):
        return pl.BlockSpec((None, TB, D), lambda b, j: (b, blk(s)(j), 0))

    def tokT(s, height):
        return pl.BlockSpec((None, height, TB), lambda b, j: (b, 0, blk(s)(j)))

    kind, per_sub, consts, const_specs = None, (), (), []
    if mixer is not None and mixer[0] == "na":
        kind, yt, w_o, b_o = mixer
        per_sub, consts = ((yt, D),), (w_o, b_o.reshape(1, D))
        const_specs = [_const_spec(w_o.shape), _const_spec((1, D))]
    elif mixer is not None:
        kind, hf, hb, pt, ml_g, w_o = mixer
        mlg = jnp.broadcast_to(ml_g.astype(F32)[:, None], (ML_INNER, 128))
        per_sub, consts = ((hf, ML_INNER), (hb, ML_INNER), (pt, ML_INNER)), (mlg, w_o)
        const_specs = [_const_spec((ML_INNER, 128)), _const_spec(w_o.shape)]
    args, specs = [], []
    for s in range(n_sub):
        args += [xs, mods] + [a for a, _ in per_sub]
        specs += [tok(s), mod_spec(s)] + [tokT(s, hgt) for _, hgt in per_sub]
    return pl.pallas_call(
        functools.partial(_ffn_kernel, n_sub=n_sub, mixer=kind),
        grid=(bsz, nblk // n_sub),
        in_specs=specs + [_const_spec((4, D))] + const_specs + [_const_spec(w1.shape), _const_spec(w2.shape)],
        out_specs=pl.BlockSpec((None, n_sub * TB, D), lambda b, j: (b, j, 0)),
        out_shape=jax.ShapeDtypeStruct((bsz, nblk * TB, D), F32),
        compiler_params=_cparams(("arbitrary", "arbitrary")),
        name="ffn" if mixer is None else kind + "_ffn",
    )(*args, ng, *consts, w1, w2)


def _gelu_tanh(x):
    return 0.5 * x * (1.0 + jnp.tanh(np.float32(np.sqrt(2.0 / np.pi)) * (x + 0.044715 * (x * x * x))))


def _gmlp_kernel(*refs, split_input):
    if split_input:
        ctx_ref, x_ref = refs[:2]
        refs = refs[1:]
        x = jnp.where(pl.program_id(1) == 0, ctx_ref[...], x_ref[...])
    else:
        x = refs[0][...]
    _, mod_ref, g_ref, win_ref, bin_ref, lng_ref, ws_ref, bst_ref, wout_ref, o_ref, v_scr, t_scr = refs
    mod = mod_ref[...]
    g = g_ref[...]
    h = _premod(x, g[0:1], mod[0:1], mod[1:2]).astype(BF16)
    v = _gelu_tanh(jnp.dot(h, win_ref[:, GM_HALF:], preferred_element_type=F32) + bin_ref[:, GM_HALF:])
    vc = v - jnp.mean(v, axis=-1, keepdims=True)
    v = vc * lax.rsqrt(jnp.mean(vc * vc, axis=-1, keepdims=True) + EPS) * lng_ref[...]
    v_scr[...] = v.astype(BF16)
    u = _gelu_tanh(jnp.dot(h, win_ref[:, :GM_HALF], preferred_element_type=F32) + bin_ref[:, :GM_HALF])
    bst = bst_ref[...]
    for n in range(TB // GM_CHUNK):
        rows = slice(n * GM_CHUNK, (n + 1) * GM_CHUNK)
        for gi in range(GM_GROUPS):
            cols = slice(gi * GM_GW, (gi + 1) * GM_GW)
            s = jnp.dot(ws_ref[gi], v_scr[rows, cols], preferred_element_type=F32) + bst[:, gi:gi + 1]
            t_scr[rows, cols] = (u[rows, cols] * s).astype(BF16)
    y = jnp.dot(t_scr[...], wout_ref[...], preferred_element_type=F32)
    o_ref[...] = x + mod[2:3] * _rms(y, g[1:2])


def _gmlp_layer(xs, mods, ng, w_in, b_in, ln_g, ws, bs, w_out, drop_ctx):
    split = isinstance(xs, tuple)
    if split:
        assert not drop_ctx
        bsz, ttot = xs[1].shape[0], xs[0].shape[1] + xs[1].shape[1]
        x_specs = [pl.BlockSpec((None, TB, D), lambda b, j: (b, 0, 0)),
                   pl.BlockSpec((None, TB, D), lambda b, j: (b, jnp.maximum(j - 1, 0), 0))]
    else:
        bsz, ttot, _ = xs.shape
        xs = (xs,)
        x_specs = [_tok_spec(D, 1 if drop_ctx else 0)]
    nblk = ttot // TB - (1 if drop_ctx else 0)
    return pl.pallas_call(
        functools.partial(_gmlp_kernel, split_input=split),
        grid=(bsz, nblk),
        in_specs=x_specs + [_mod_spec(not drop_ctx), _const_spec((4, D)),
                            _const_spec(w_in.shape), _const_spec((1, 2 * GM_HALF)), _const_spec((1, GM_HALF)),
                            _const_spec(ws.shape), _const_spec((GM_CHUNK, GM_GROUPS)), _const_spec(w_out.shape)],
        out_specs=_tok_spec(D),
        out_shape=jax.ShapeDtypeStruct((bsz, nblk * TB, D), F32),
        scratch_shapes=[pltpu.VMEM((TB, GM_HALF), BF16), pltpu.VMEM((TB, GM_HALF), BF16)],
        compiler_params=_cparams(("arbitrary", "arbitrary")),
        name="gmlp",
    )(*xs, mods, ng, w_in, b_in.reshape(1, -1), ln_g.reshape(1, -1), ws, bs.T, w_out)


def _na_inproj_kernel(x_ref, mod_ref, g_ref, wk_ref, bk_ref, wt_ref, bt_ref, k_ref, pt_ref):
    mod = mod_ref[...]
    g = g_ref[...]
    h = _premod(x_ref[...], g[0:1], mod[0:1], mod[1:2]).astype(BF16)
    k_ref[...] = (jnp.dot(h, wk_ref[...], preferred_element_type=F32) + bk_ref[...]).astype(BF16)
    bt = bt_ref[...]
    pt = lax.dot_general(wt_ref[...], h, DN_T, preferred_element_type=F32) + jnp.concatenate([bt, bt], axis=1)
    pt_ref[:D, :] = (pt[:D] * NA_QSCALE).astype(BF16)
    pt_ref[D:, :] = pt[D:].astype(BF16)


def _na_inproj(xs, mods, ng, w_qkv, b_qkv):
    bsz, ttot, _ = xs.shape
    wk = w_qkv[:, D:2 * D]
    wt = jnp.concatenate([w_qkv[:, :D], w_qkv[:, 2 * D:]], axis=1).T
    bt = jnp.concatenate([b_qkv[:D], b_qkv[2 * D:]]).astype(F32)
    bt = jnp.broadcast_to(bt[:, None], (2 * D, 128))
    return pl.pallas_call(
        _na_inproj_kernel,
        grid=(bsz, ttot // TB),
        in_specs=[_tok_spec(D), _mod_spec(), _const_spec((4, D)), _const_spec((D, D)), _const_spec((1, D)),
                  _const_spec((2 * D, D)), _const_spec((2 * D, 128))],
        out_specs=[_tok_spec(D), _tokT_spec(2 * D)],
        out_shape=[jax.ShapeDtypeStruct((bsz, ttot, D), BF16),
                   jax.ShapeDtypeStruct((bsz, 2 * D, ttot), BF16)],
        compiler_params=_cparams(("arbitrary", "arbitrary")),
        name="na_inproj",
    )(xs, mods, ng, wk, b_qkv[D:2 * D].reshape(1, D), wt, bt)


def _na_bias_table(rpb, rows):
    assert rows >= 4 * NA_RB and rows % NA_RB == 0
    rr = np.arange(NA_RB)
    ib = np.arange(NA_KR)
    qc = np.arange(GRID_W)
    kc = np.arange(GRID_W)
    ws = np.clip(qc - NA_KW // 2, 0, GRID_W - NA_KW)
    valid_col = (kc[None, :] >= ws[:, None]) & (kc[None, :] < ws[:, None] + NA_KW)
    dcol = np.clip(kc[None, :] - qc[:, None], -(NA_KW - 1), NA_KW - 1) + NA_KW - 1
    tiles = jnp.where(jnp.asarray(valid_col)[None, None], rpb[:, :, dcol].astype(F32) * LOG2E, NEG)
    drows, valids = [], []
    for off, rs_minus_r in ((rr, np.full(NA_RB, -(NA_KH // 2))),
                            (np.full(NA_RB, NA_RB), -rr),
                            (np.zeros(NA_RB, np.int64), -(NA_KH // 2) - rr)):
        i = ib[None, :] - off[:, None]
        valids.append((i >= 0) & (i < NA_KH))
        drows.append(np.clip(rs_minus_r[:, None] + i + NA_KH - 1, 0, 2 * NA_KH - 2))
    drow = np.stack(drows)
    valid_row = np.stack(valids)
    t = tiles[:, drow]
    t = jnp.where(jnp.asarray(valid_row)[None, :, :, :, None, None], t, NEG)
    t = t.reshape(NA_HEADS // 2, 2, 3, NA_RB, NA_KR, GRID_W, GRID_W)
    t = jnp.transpose(t, (2, 0, 4, 6, 1, 3, 5))
    return t.reshape(3, NA_HEADS // 2, NA_KR * GRID_W, 2 * TB)


def _na_kernel(q_ref, kp_ref, kc_ref, kn_ref, kx_ref, vp_ref, vc_ref, vn_ref, vx_ref, bias_ref, o_ref):
    j = pl.program_id(2)
    row = lax.broadcasted_iota(jnp.int32, (2 * NA_DH, 1), 0)

    def heads(krefs, vrefs, nbias):
        def scores(pair):
            blk = slice(pair * 2 * NA_DH, (pair + 1) * 2 * NA_DH)
            q = q_ref[blk, :]
            zq = jnp.zeros_like(q)
            q2 = jnp.concatenate([jnp.where(row < NA_DH, q, zq), jnp.where(row < NA_DH, zq, q)], axis=1)
            kcat = jnp.concatenate([kr[:, blk] for kr in krefs], axis=0)
            return jnp.dot(kcat, q2, preferred_element_type=F32)

        def finish(pair, s):
            blk = slice(pair * 2 * NA_DH, (pair + 1) * 2 * NA_DH)
            ss = [s[i * TB:(i + 1) * TB] + bias_ref[pair, i * TB:(i + 1) * TB, :] if i < nbias
                  else s[i * TB:(i + 1) * TB] for i in range(len(krefs))]
            m = functools.reduce(jnp.maximum, [jnp.max(s, axis=0, keepdims=True) for s in ss])
            ps = [jnp.exp2(s - m) for s in ss]
            l = functools.reduce(jnp.add, [jnp.sum(p, axis=0, keepdims=True) for p in ps])
            vcat = jnp.concatenate([vr[blk, :] for vr in vrefs], axis=1)
            pcat = jnp.concatenate([p.astype(BF16) for p in ps], axis=0)
            o = jnp.dot(vcat, pcat, preferred_element_type=F32) * (1.0 / l)
            o_ref[blk, :] = jnp.where(row < NA_DH, o[:, :TB], o[:, TB:]).astype(BF16)

        s_next = scores(0)
        for pair in range(NA_PAIRS):
            s_cur = s_next
            if pair + 1 < NA_PAIRS:
                s_next = scores(pair + 1)
            finish(pair, s_cur)

    @pl.when(j == 0)
    def _():
        heads((kx_ref,), (vx_ref,), 0)

    @pl.when(j > 0)
    def _():
        heads((kp_ref, kc_ref, kn_ref, kx_ref), (vp_ref, vc_ref, vn_ref, vx_ref), 3)


def _na_attention(kt, pt, bias):
    bsz, ttot, _ = kt.shape
    nblk = ttot // TB
    hp = NA_HEADS // 2 // NA_PAIRS
    wb = 2 * NA_DH * NA_PAIRS

    def tok(fn):
        return pl.BlockSpec((None, TB, wb), lambda h, b, j: (b, fn(j), h))

    def tokT(row0, fn):
        return pl.BlockSpec((None, wb, TB), lambda h, b, j: (b, row0 + h, fn(j)))

    prev = lambda j: jnp.clip(j - 1, 1, nblk - 1)
    cur = lambda j: j
    nxt = lambda j: jnp.clip(j + 1, 1, nblk - 1)
    zero = lambda j: 0
    case = lambda j: jnp.where(j == 1, 1, jnp.where(j == nblk - 1, 2, 0))
    return pl.pallas_call(
        _na_kernel,
        grid=(hp, bsz, nblk),
        in_specs=[tokT(0, cur),
                  tok(prev), tok(cur), tok(nxt), tok(zero),
                  tokT(hp, prev), tokT(hp, cur), tokT(hp, nxt), tokT(hp, zero),
                  pl.BlockSpec((None, NA_PAIRS, NA_KR * GRID_W, 2 * TB), lambda h, b, j: (case(j), h, 0, 0))],
        out_specs=tokT(0, cur),
        out_shape=jax.ShapeDtypeStruct((bsz, D, ttot), BF16),
        compiler_params=_cparams(("arbitrary", "arbitrary", "arbitrary")),
        name="na_attention",
    )(pt, kt, kt, kt, kt, pt, pt, pt, pt, bias)


def _na_layer(xs, mods, ng, w_qkv, b_qkv, rpb, w_o, b_o):
    rows = (xs.shape[1] - TB) // GRID_W
    kt, pt = _na_inproj(xs, mods, ng, w_qkv, b_qkv)
    att = _na_attention(kt, pt, _na_bias_table(rpb, rows))
    return ("na", att, w_o, b_o)


def _rope_tables(t_lat):
    pos = jnp.arange(t_lat)
    row = (pos // GRID_W).astype(F32)
    col = (pos % GRID_W).astype(F32)
    d_axis = ML_DK // 2
    inv = ROPE_BASE ** (-jnp.arange(0, d_axis, 2, dtype=F32) / d_axis)
    lane = np.arange(2 * ML_DK)
    d = lane % ML_DK
    use_col = (d // d_axis) == 1
    jdx = d % (d_axis // 2)
    first = (d % d_axis) < d_axis // 2
    ang_r = row[:, None] * inv[None, :]
    ang_c = col[:, None] * inv[None, :]
    ang = jnp.where(jnp.asarray(use_col)[None, :], ang_c[:, jdx], ang_r[:, jdx])
    cos = jnp.cos(ang)
    sin = jnp.sin(ang)
    ss = jnp.where(jnp.asarray(first)[None, :], -sin, sin)
    cos = jnp.concatenate([jnp.ones((TB, 2 * ML_DK), F32), cos], axis=0)
    ss = jnp.concatenate([jnp.zeros((TB, 2 * ML_DK), F32), ss], axis=0)
    return cos, ss


def _log_sigmoid(x):
    return jnp.minimum(x, 0.0) - jnp.log(1.0 + jnp.exp(-jnp.abs(x)))


def _lane_scan(x, op, ident, reverse):
    n = x.shape[1]
    lane = lax.broadcasted_iota(jnp.int32, (1, n), 1)
    d = 1
    while d < n:
        if reverse:
            sh = jnp.where(lane < n - d, pltpu.roll(x, n - d, 1), ident)
        else:
            sh = jnp.where(lane >= d, pltpu.roll(x, d, 1), ident)
        x = op(x, sh)
        d *= 2
    return x


def _ml_inproj_kernel(x_ref, mod_ref, g_ref, wk_ref, wt_ref, cos_ref, ss_ref, cost_ref, sst_ref,
                      wg_ref, bg_ref, k_ref, pt_ref, gr_ref, gc_ref):
    mod = mod_ref[...]
    g = g_ref[...]
    h = _premod(x_ref[...], g[0:1], mod[0:1], mod[1:2]).astype(BF16)
    half = ML_DK // 4

    nh = ML_HEADS
    bg = bg_ref[...]
    gr = lax.dot_general(wg_ref[...], h, DN_T, preferred_element_type=F32) + jnp.concatenate([bg, bg], axis=1)
    outs = []
    for dirn in range(2):
        ig = gr[2 * dirn * nh:(2 * dirn + 1) * nh]
        lf = _log_sigmoid(gr[(2 * dirn + 1) * nh:(2 * dirn + 2) * nh])
        b = _lane_scan(lf, jnp.add, 0.0, dirn == 1)
        u = ig - b
        outs += [u, _lane_scan(u, jnp.maximum, -jnp.inf, dirn == 1), b]
    gr_ref[...] = jnp.concatenate(outs, axis=0)
    ucols = jnp.concatenate([outs[0], outs[3], jnp.zeros((128 - 2 * nh, TB), F32)], axis=0)
    gc_ref[...] = ucols.T

    k = jnp.dot(h, wk_ref[...], preferred_element_type=F32)
    cos = cos_ref[...]
    ss = ss_ref[...]
    lane = lax.broadcasted_iota(jnp.int32, (1, 2 * ML_DK), 1)
    first_l = (lane % (2 * half)) < half
    for blk in range(ML_QK // 128):
        cols = slice(blk * 128, (blk + 1) * 128)
        xb = k[:, cols]
        sw = jnp.where(first_l, pltpu.roll(xb, 128 - half, 1), pltpu.roll(xb, half, 1))
        k_ref[:, cols] = ((xb * cos + sw * ss) * (ML_DK ** -0.5)).astype(BF16)

    pt = lax.dot_general(wt_ref[...], h, DN_T, preferred_element_type=F32)
    pt_ref[:2 * ML_INNER, :] = pt[:2 * ML_INNER].astype(BF16)
    cost = cost_ref[...]
    sst = sst_ref[...]
    row = lax.broadcasted_iota(jnp.int32, (2 * ML_DK, 1), 0)
    first_r = (row % (2 * half)) < half
    for blk in range(ML_QK // 128):
        rows = slice(2 * ML_INNER + blk * 128, 2 * ML_INNER + (blk + 1) * 128)
        xb = pt[rows]
        sw = jnp.where(first_r, pltpu.roll(xb, 128 - half, 0), pltpu.roll(xb, half, 0))
        pt_ref[rows, :] = (xb * cost + sw * sst).astype(BF16)


def _ml_inproj(xs, mods, ng, w_in, w_gate, b_gate):
    bsz, ttot, _ = xs.shape
    ngate = 4 * ML_HEADS
    cos, ss = _rope_tables(ttot - TB)
    wk = w_in[:, ML_QK:2 * ML_QK]
    wt = jnp.concatenate([w_in[:, 2 * ML_QK + ML_INNER:], w_in[:, 2 * ML_QK:2 * ML_QK + ML_INNER],
                          w_in[:, :ML_QK]], axis=1).T
    wg = jnp.concatenate([w_gate[0], w_gate[1]], axis=1).astype(BF16).T
    bg = jnp.broadcast_to(b_gate.reshape(ngate, 1).astype(F32), (ngate, 128))
    nt = wt.shape[0]
    tab = pl.BlockSpec((TB, 2 * ML_DK), lambda b, j: (j, 0))
    tabT = pl.BlockSpec((2 * ML_DK, TB), lambda b, j: (0, j))
    return pl.pallas_call(
        _ml_inproj_kernel,
        grid=(bsz, ttot // TB),
        in_specs=[_tok_spec(D), _mod_spec(), _const_spec((4, D)), _const_spec(wk.shape), _const_spec(wt.shape),
                  tab, tab, tabT, tabT, _const_spec((ngate, D)), _const_spec((ngate, 128))],
        out_specs=[_tok_spec(ML_QK), _tokT_spec(nt), _tokT_spec(6 * ML_HEADS), _tok_spec(128)],
        out_shape=[jax.ShapeDtypeStruct((bsz, ttot, ML_QK), BF16),
                   jax.ShapeDtypeStruct((bsz, nt, ttot), BF16),
                   jax.ShapeDtypeStruct((bsz, 6 * ML_HEADS, ttot), F32),
                   jax.ShapeDtypeStruct((bsz, ttot, 128), F32)],
        compiler_params=_cparams(("arbitrary", "arbitrary")),
        name="ml_inproj",
    )(xs, mods, ng, wk, wt, cos, ss, cos.T, ss.T, wg, bg)


def _ml_scan_kernel(kf_ref, vf_ref, qf_ref, grf_ref, gcf_ref, kb_ref, vb_ref, qb_ref, grb_ref, gcb_ref,
                    hf_ref, hb_ref, s_scr, m_scr):
    step = pl.program_id(1)

    @pl.when(step == 0)
    def _():
        s_scr[...] = jnp.zeros_like(s_scr)
        m_scr[...] = jnp.zeros_like(m_scr)

    nh = ML_HEADS
    lane = lax.broadcasted_iota(jnp.int32, (1, 2 * ML_DK), 1)
    r_i = lax.broadcasted_iota(jnp.int32, (TB, TB), 0)
    c_i = lax.broadcasted_iota(jnp.int32, (TB, TB), 1)
    ones_rows = jnp.ones((ML_ONES, TB), BF16)

    dirs = ((kf_ref, vf_ref, qf_ref, grf_ref, gcf_ref, hf_ref, r_i <= c_i, TB - 1),
            (kb_ref, vb_ref, qb_ref, grb_ref, gcb_ref, hb_ref, r_i >= c_i, 0))
    stats = []
    for dirn, (_, _, _, gr_ref, _, _, _, last) in enumerate(dirs):
        u = gr_ref[0:nh, :]
        pm = gr_ref[nh:2 * nh, :]
        b = gr_ref[2 * nh:3 * nh, :]
        m_st = m_scr[dirn][:, 0:1]
        mrow = jnp.maximum(m_st, pm)
        m_last = mrow[:, last:last + 1]
        stats.append(dict(mrow=mrow, inter=jnp.exp(m_st - mrow), em=jnp.exp(-(b + mrow)),
                          wrow=jnp.exp(u - m_last), decay=jnp.exp(m_st - m_last)))
        m_scr[dirn] = jnp.broadcast_to(b[:, last:last + 1] + m_last, (nh, 128))

    def stage_a(i):
        dirn, hd = divmod(i, nh)
        k_ref, v_ref, q_ref = dirs[dirn][:3]
        pair, a = divmod(hd, 2)
        kp = k_ref[:, pair * 128:(pair + 1) * 128]
        sel = (lane >= a * ML_DK) & (lane < (a + 1) * ML_DK)
        km = jnp.where(sel, kp, jnp.zeros_like(kp))
        qt = q_ref[pair * 128:(pair + 1) * 128, :]
        vaug = jnp.concatenate([v_ref[hd * ML_DV:(hd + 1) * ML_DV, :], ones_rows], axis=0)
        st = s_scr[i]
        s_t = jnp.dot(km, qt, preferred_element_type=F32)
        r2 = jnp.dot(st.astype(BF16), qt, preferred_element_type=F32)
        return km, vaug, st, s_t, r2

    def stage_b(i, km, vaug, st, s_t, r2):
        dirn, hd = divmod(i, nh)
        gc_ref, out_ref, tri = dirs[dirn][4:7]
        sd = stats[dirn]
        u_col = gc_ref[:, i:i + 1]
        d_t = jnp.where(tri, jnp.exp(u_col - sd["mrow"][hd:hd + 1, :]), 0.0)
        a_t = (s_t * d_t).astype(BF16)
        res = jnp.dot(vaug, a_t, preferred_element_type=F32) + sd["inter"][hd:hd + 1, :] * r2
        den = jnp.maximum(jnp.abs(res[ML_DV:ML_DV + 1, :]), sd["em"][hd:hd + 1, :])
        out_ref[hd * ML_DV:(hd + 1) * ML_DV, :] = res[:ML_DV] * (1.0 / den)
        vw = (vaug.astype(F32) * sd["wrow"][hd:hd + 1, :]).astype(BF16)
        s_scr[i] = sd["decay"][hd:hd + 1, :] * st + jnp.dot(vw, km, preferred_element_type=F32)

    nxt = stage_a(0)
    for i in range(2 * nh):
        cur = nxt
        if i + 1 < 2 * nh:
            nxt = stage_a(i + 1)
        stage_b(i, *cur)


def _ml_scan(kt, pt, grow, gcol):
    bsz, ttot, _ = kt.shape
    nblk = ttot // TB
    fwd = lambda i: i
    bwd = lambda i: jnp.where(i == 0, 0, nblk - i)
    vrow = ML_INNER // ML_INNER
    qrow = 2 * ML_INNER // ML_QK

    def specs(fn, dirn):
        return [pl.BlockSpec((None, TB, ML_QK), lambda b, i: (b, fn(i), 0)),
                pl.BlockSpec((None, ML_INNER, TB), lambda b, i: (b, vrow, fn(i))),
                pl.BlockSpec((None, ML_QK, TB), lambda b, i: (b, qrow, fn(i))),
                pl.BlockSpec((None, 3 * ML_HEADS, TB), lambda b, i: (b, dirn, fn(i))),
                pl.BlockSpec((None, TB, 128), lambda b, i: (b, fn(i), 0))]

    return pl.pallas_call(
        _ml_scan_kernel,
        grid=(bsz, nblk),
        in_specs=specs(fwd, 0) + specs(bwd, 1),
        out_specs=[pl.BlockSpec((None, ML_INNER, TB), lambda b, i: (b, 0, fwd(i))),
                   pl.BlockSpec((None, ML_INNER, TB), lambda b, i: (b, 0, bwd(i)))],
        out_shape=[jax.ShapeDtypeStruct((bsz, ML_INNER, ttot), F32)] * 2,
        scratch_shapes=[pltpu.VMEM((2 * ML_HEADS, ML_DV + ML_ONES, 2 * ML_DK), F32),
                        pltpu.VMEM((2, ML_HEADS, 128), F32)],
        compiler_params=_cparams(("arbitrary", "arbitrary")),
        name="ml_scan",
    )(kt, pt, pt, grow, gcol, kt, pt, pt, grow, gcol)


def _ml_layer(xs, mods, ng, w_in, w_gate, b_gate, ml_g, w_out):
    kt, pt, grow, gcol = _ml_inproj(xs, mods, ng, w_in, w_gate, b_gate)
    hf, hb = _ml_scan(kt, pt, grow, gcol)
    return ("ml", hf, hb, pt, ml_g, w_out)


def kernel(x, c, ctx, c_ctx, ada_w, ada_b, norm_g, ffn_w1, ffn_w2, gm_w_in, gm_b_in, gm_ln_g, gm_ws, gm_bs, gm_w_out, na_w_qkv, na_b_qkv, na_rpb, na_w_o, na_b_o, ml_w_in, ml_w_gate, ml_b_gate, ml_norm_g, ml_w_out):
    depth = ada_w.shape[0]
    assert ctx.shape[1] == TB and x.shape[1] % TB == 0 and x.shape[0] <= CTX_ROW
    mods_all = _mods_all(c, c_ctx, ada_w, ada_b)
    xs = (ctx, x) if depth > 1 else jnp.concatenate([ctx, x], axis=1)
    for i in range(depth):
        kind, j = i % 3, i // 3
        mods = mods_all[i]
        ng = norm_g[i]
        last = i == depth - 1
        mixer = None
        if kind == 0:
            xs = _gmlp_layer(xs, mods, ng, gm_w_in[j].astype(BF16), gm_b_in[j], gm_ln_g[j],
                             gm_ws[j].astype(BF16), gm_bs[j], gm_w_out[j].astype(BF16), last)
        elif kind == 1:
            mixer = _na_layer(xs, mods, ng, na_w_qkv[j].astype(BF16), na_b_qkv[j], na_rpb[j],
                              na_w_o[j].astype(BF16), na_b_o[j])
        else:
            mixer = _ml_layer(xs, mods, ng, ml_w_in[j].astype(BF16), ml_w_gate[j], ml_b_gate[j],
                              ml_norm_g[j], ml_w_out[j].astype(BF16))
        has_ctx = xs.shape[1] != x.shape[1]
        xs = _ffn_layer(xs, mods, ng, ffn_w1[i].astype(BF16), ffn_w2[i].astype(BF16), has_ctx, has_ctx and last,
                        mixer)
    return xs
```

```python
import functools

import numpy as np
import jax
import jax.numpy as jnp
from jax import lax
from jax.experimental import pallas as pl
from jax.experimental.pallas import tpu as pltpu

F32 = jnp.float32
BF16 = jnp.bfloat16

D = 1024
TB = 256
GRID_W = 64
EPS = 1e-6
CTX_ROW = 8
MOD_ROWS = 16
GM_HALF = 3 * D
GM_CHUNK = 128
GM_GROUPS = 8
GM_GW = GM_HALF // GM_GROUPS
NA_HEADS = 16
NA_DH = 64
NA_KH = 8
NA_KW = 16
NA_RB = TB // GRID_W
NA_KR = 3 * NA_RB
NEG = -1e30
LOG2E = float(np.log2(np.e))
NA_QSCALE = NA_DH ** -0.5 * LOG2E
NA_PAIRS = 4
ML_HEADS = 8
ML_DK = 64
ML_DV = 128
ML_QK = ML_HEADS * ML_DK
ML_INNER = ML_HEADS * ML_DV
ML_ONES = 16
ROPE_BASE = 10000.0

VMEM_LIMIT = 56 * 1024 * 1024

DN_T = (((1,), (1,)), ((), ()))
DN_0 = (((0,), (0,)), ((), ()))


def _cparams(sem, flags=None):
    return pltpu.CompilerParams(dimension_semantics=sem, vmem_limit_bytes=VMEM_LIMIT, flags=flags)


def _rms(x, g):
    ms = jnp.mean(x * x, axis=-1, keepdims=True)
    return x * lax.rsqrt(ms + EPS) * g


def _premod(x, g, shift, scale):
    return _rms(x, g) * (1.0 + scale) + shift


def _const_spec(shape):
    nd = len(shape)
    return pl.BlockSpec(shape, lambda *_: (0,) * nd, pipeline_mode=pl.Buffered(1))


def _tok_spec(width, off=0, col=0):
    return pl.BlockSpec((None, TB, width), lambda b, j: (b, j + off, col))


def _tokT_spec(height):
    return pl.BlockSpec((None, None, height, TB), lambda b, j: (b, j, 0, 0))


def _mod_spec(ctx_first=True):
    if ctx_first:
        return pl.BlockSpec((None, 6, D), lambda b, j: (jnp.where(j == 0, CTX_ROW, b), 0, 0))
    return pl.BlockSpec((None, 6, D), lambda b, j: (b, 0, 0))


def _mod_kernel(c_ref, w_ref, b_ref, o_ref):
    c = c_ref[...]
    a = (c * jax.nn.sigmoid(c)).astype(BF16)
    o_ref[...] = jnp.dot(a, w_ref[...].astype(BF16), preferred_element_type=F32) + b_ref[...]


def _mods_all(c, c_ctx, ada_w, ada_b):
    depth = ada_w.shape[0]
    n = ada_w.shape[2]
    tn = 1536
    cv = jnp.zeros((MOD_ROWS, D), F32).at[:c.shape[0]].set(c).at[CTX_ROW].set(c_ctx)
    out = pl.pallas_call(
        _mod_kernel,
        grid=(depth, n // tn),
        in_specs=[pl.BlockSpec((MOD_ROWS, D), lambda l, k: (0, 0)),
                  pl.BlockSpec((None, D, tn), lambda l, k: (l, 0, k)),
                  pl.BlockSpec((None, 1, tn), lambda l, k: (l, 0, k))],
        out_specs=pl.BlockSpec((None, MOD_ROWS, tn), lambda l, k: (l, 0, k)),
        out_shape=jax.ShapeDtypeStruct((depth, MOD_ROWS, n), F32),
        compiler_params=_cparams(("arbitrary", "arbitrary")),
        name="adaln_mod",
    )(cv, ada_w, ada_b.reshape(depth, 1, n))
    return out.reshape(depth, MOD_ROWS, 6, D)


def _software_pipeline(n_items, stages):
    carry = [None] * n_items
    for t in range(n_items + len(stages) - 1):
        for k, stage in enumerate(stages):
            if 0 <= t - k < n_items:
                carry[t - k] = stage(t - k, carry[t - k])


def _sub_plan(bsz, nblk, off, over_batch):
    if over_batch:
        n = 2 if bsz % 2 == 0 else 1
        return (n, (bsz // n, nblk), lambda s: (lambda b, j: n * b + s), lambda s: (lambda b, j: off + j),
                pl.BlockSpec((n, TB, D), lambda b, j: (b, j, 0)))
    n = 3 if nblk % 3 == 0 else (2 if nblk % 2 == 0 else 1)
    return (n, (bsz, nblk // n), lambda s: (lambda b, j: b), lambda s: (lambda b, j: off + n * j + s),
            pl.BlockSpec((None, n * TB, D), lambda b, j: (b, j, 0)))


def _store_sub(o_ref, s, over_batch, val):
    if over_batch:
        o_ref[s] = val
    else:
        o_ref[s * TB:(s + 1) * TB, :] = val


def _ffn_kernel(*refs, n_sub, over_batch, mixer):
    per = {None: 2, "na": 3, "ml": 5}[mixer]
    subs = [refs[s * per:(s + 1) * per] for s in range(n_sub)]
    consts, o_ref = refs[n_sub * per:-1], refs[-1]
    g = consts[0][...]
    w1_ref, w2_ref = consts[-2:]

    def mixer_proj(s, _):
        if mixer == "na":
            wo_ref, bo_ref = consts[1:3]
            return lax.dot_general(subs[s][2][...], wo_ref[...], DN_0, preferred_element_type=F32) + bo_ref[...]
        hf_ref, hb_ref, og_ref = subs[s][2:5]
        mlg_ref, wo_ref = consts[1:3]
        hs = hf_ref[...] + hb_ref[...]
        parts = []
        for hd in range(ML_HEADS):
            xh = hs[hd * ML_DV:(hd + 1) * ML_DV]
            parts.append(xh * lax.rsqrt(jnp.mean(xh * xh, axis=0, keepdims=True) + EPS))
        mlg = mlg_ref[...]
        hh = jnp.concatenate(parts, axis=0) * jnp.concatenate([mlg, mlg], axis=1)
        yin = (hh * jax.nn.sigmoid(og_ref[...].astype(F32))).astype(BF16)
        return lax.dot_general(yin, wo_ref[...], DN_0, preferred_element_type=F32)

    def up(s, ymix):
        x = subs[s][0][...]
        mod = subs[s][1][...]
        if mixer is not None:
            x = x + mod[2:3] * _rms(ymix, g[1:2])
        h = _premod(x, g[2:3], mod[3:4], mod[4:5]).astype(BF16)
        return x, jnp.dot(h, w1_ref[...], preferred_element_type=F32)

    def down(s, c):
        a = jnp.square(jnp.maximum(c[1], 0.0)).astype(BF16)
        return c[0], jnp.dot(a, w2_ref[...], preferred_element_type=F32)

    def finish(s, c):
        mod = subs[s][1][...]
        _store_sub(o_ref, s, over_batch, c[0] + mod[5:6] * _rms(c[1], g[3:4]))

    _software_pipeline(n_sub, ([] if mixer is None else [mixer_proj]) + [up, down, finish])


def _ffn_layer(xs, mods, ng, w1, w2, has_ctx, drop_ctx, mixer=None):
    bsz, ttot, _ = xs.shape
    off = 1 if drop_ctx else 0
    nblk = ttot // TB - off
    over_batch = mixer is not None and mixer[0] == "ml"
    n_sub, grid, b_of, j_of, out_spec = _sub_plan(bsz, nblk, off, over_batch)
    ctx_first = has_ctx and not drop_ctx

    def mod_spec(s):
        bi, ji = b_of(s), j_of(s)
        row = (lambda b, j: jnp.where(ji(b, j) == 0, CTX_ROW, bi(b, j))) if ctx_first else bi
        return pl.BlockSpec((None, 6, D), lambda b, j: (row(b, j), 0, 0))

    def tok(s):
        bi, ji = b_of(s), j_of(s)
        return pl.BlockSpec((None, TB, D), lambda b, j: (bi(b, j), ji(b, j), 0))

    def tokT(s, height):
        bi, ji = b_of(s), j_of(s)
        return pl.BlockSpec((None, None, height, TB), lambda b, j: (bi(b, j), ji(b, j), 0, 0))

    kind, per_sub, consts, const_specs = None, (), (), []
    if mixer is not None and mixer[0] == "na":
        kind, yt, w_o, b_o = mixer
        per_sub, consts = ((yt, D),), (w_o, b_o.reshape(1, D))
        const_specs = [_const_spec(w_o.shape), _const_spec((1, D))]
    elif mixer is not None:
        kind, hf, hb, pt, ml_g, w_o = mixer
        mlg = jnp.broadcast_to(ml_g.astype(F32)[:, None], (ML_INNER, 128))
        per_sub, consts = ((hf, ML_INNER), (hb, ML_INNER), (pt, ML_INNER)), (mlg, w_o)
        const_specs = [_const_spec((ML_INNER, 128)), _const_spec(w_o.shape)]
    args, specs = [], []
    for s in range(n_sub):
        args += [xs, mods] + [a for a, _ in per_sub]
        specs += [tok(s), mod_spec(s)] + [tokT(s, hgt) for _, hgt in per_sub]
    return pl.pallas_call(
        functools.partial(_ffn_kernel, n_sub=n_sub, over_batch=over_batch, mixer=kind),
        grid=grid,
        in_specs=specs + [_const_spec((4, D))] + const_specs + [_const_spec(w1.shape), _const_spec(w2.shape)],
        out_specs=out_spec,
        out_shape=jax.ShapeDtypeStruct((bsz, nblk * TB, D), F32),
        compiler_params=_cparams(("arbitrary", "arbitrary")),
        name="ffn" if mixer is None else kind + "_ffn",
    )(*args, ng, *consts, w1, w2)


def _gelu_tanh(x):
    return 0.5 * x * (1.0 + jnp.tanh(np.float32(np.sqrt(2.0 / np.pi)) * (x + 0.044715 * (x * x * x))))


def _gmlp_kernel(*refs, n_sub, split_input):
    per = 3 if split_input else 2
    subs = [refs[s * per:(s + 1) * per] for s in range(n_sub)]
    g_ref, win_ref, bin_ref, lng_ref, ws_ref, bst_ref, wout_ref, o_ref, v_scr, t_scr = refs[n_sub * per:]
    g = g_ref[...]

    def v_proj(s, _):
        if split_input:
            x = jnp.where(pl.program_id(1) == 0, subs[s][0][...], subs[s][1][...])
        else:
            x = subs[s][0][...]
        mod = subs[s][-1][...]
        h = _premod(x, g[0:1], mod[0:1], mod[1:2]).astype(BF16)
        return x, h, jnp.dot(h, win_ref[:, GM_HALF:], preferred_element_type=F32) + bin_ref[:, GM_HALF:]

    def u_proj(s, c):
        x, h, zv = c
        v = _gelu_tanh(zv)
        vc = v - jnp.mean(v, axis=-1, keepdims=True)
        v = vc * lax.rsqrt(jnp.mean(vc * vc, axis=-1, keepdims=True) + EPS) * lng_ref[...]
        v_scr[s] = v.astype(BF16)
        return x, jnp.dot(h, win_ref[:, :GM_HALF], preferred_element_type=F32) + bin_ref[:, :GM_HALF]

    def mix(s, c):
        x, zu = c
        u = _gelu_tanh(zu)
        bst = bst_ref[...]
        for n in range(TB // GM_CHUNK):
            rows = slice(n * GM_CHUNK, (n + 1) * GM_CHUNK)
            for gi in range(GM_GROUPS):
                cols = slice(gi * GM_GW, (gi + 1) * GM_GW)
                sp = jnp.dot(ws_ref[gi], v_scr[s, rows, cols], preferred_element_type=F32) + bst[:, gi:gi + 1]
                t_scr[s, rows, cols] = (u[rows, cols] * sp).astype(BF16)
        return x, jnp.dot(t_scr[s], wout_ref[...], preferred_element_type=F32)

    def finish(s, c):
        mod = subs[s][-1][...]
        o_ref[s] = c[0] + mod[2:3] * _rms(c[1], g[1:2])

    _software_pipeline(n_sub, [v_proj, u_proj, mix, finish])


def _gmlp_layer(xs, mods, ng, w_in, b_in, ln_g, ws, bs, w_out, drop_ctx):
    split = isinstance(xs, tuple)
    if split:
        assert not drop_ctx
        bsz, ttot = xs[1].shape[0], xs[0].shape[1] + xs[1].shape[1]
    else:
        bsz, ttot, _ = xs.shape
        xs = (xs,)
    off = 1 if drop_ctx else 0
    nblk = ttot // TB - off
    n_sub, grid, b_of, j_of, out_spec = _sub_plan(bsz, nblk, off, True)
    args, specs = [], []
    for s in range(n_sub):
        bi, ji = b_of(s), j_of(s)
        if split:
            specs += [pl.BlockSpec((None, TB, D), lambda b, j, bi=bi: (bi(b, j), 0, 0)),
                      pl.BlockSpec((None, TB, D), lambda b, j, bi=bi: (bi(b, j), jnp.maximum(j - 1, 0), 0))]
        else:
            specs += [pl.BlockSpec((None, TB, D), lambda b, j, bi=bi, ji=ji: (bi(b, j), ji(b, j), 0))]
        row = (lambda b, j, bi=bi: bi(b, j)) if drop_ctx else (lambda b, j, bi=bi: jnp.where(j == 0, CTX_ROW, bi(b, j)))
        specs += [pl.BlockSpec((None, 6, D), lambda b, j, row=row: (row(b, j), 0, 0))]
        args += list(xs) + [mods]
    return pl.pallas_call(
        functools.partial(_gmlp_kernel, n_sub=n_sub, split_input=split),
        grid=grid,
        in_specs=specs + [_const_spec((4, D)),
                          _const_spec(w_in.shape), _const_spec((1, 2 * GM_HALF)), _const_spec((1, GM_HALF)),
                          _const_spec(ws.shape), _const_spec((GM_CHUNK, GM_GROUPS)), _const_spec(w_out.shape)],
        out_specs=out_spec,
        out_shape=jax.ShapeDtypeStruct((bsz, nblk * TB, D), F32),
        scratch_shapes=[pltpu.VMEM((n_sub, TB, GM_HALF), BF16), pltpu.VMEM((n_sub, TB, GM_HALF), BF16)],
        compiler_params=_cparams(("arbitrary", "arbitrary")),
        name="gmlp",
    )(*args, ng, w_in, b_in.reshape(1, -1), ln_g.reshape(1, -1), ws, bs.T, w_out)


def _na_inproj_kernel(x_ref, mod_ref, g_ref, wk_ref, bk_ref, wt_ref, bt_ref, k_ref, pt_ref):
    mod = mod_ref[...]
    g = g_ref[...]
    h = _premod(x_ref[...], g[0:1], mod[0:1], mod[1:2]).astype(BF16)
    k_ref[...] = (jnp.dot(h, wk_ref[...], preferred_element_type=F32) + bk_ref[...]).astype(BF16)
    bt = bt_ref[...]
    pt = lax.dot_general(wt_ref[...], h, DN_T, preferred_element_type=F32) + jnp.concatenate([bt, bt], axis=1)
    pt_ref[:D, :] = (pt[:D] * NA_QSCALE).astype(BF16)
    pt_ref[D:, :] = pt[D:].astype(BF16)


def _na_inproj(xs, mods, ng, w_qkv, b_qkv):
    bsz, ttot, _ = xs.shape
    wk = w_qkv[:, D:2 * D]
    wt = jnp.concatenate([w_qkv[:, :D], w_qkv[:, 2 * D:]], axis=1).T
    bt = jnp.concatenate([b_qkv[:D], b_qkv[2 * D:]]).astype(F32)
    bt = jnp.broadcast_to(bt[:, None], (2 * D, 128))
    return pl.pallas_call(
        _na_inproj_kernel,
        grid=(bsz, ttot // TB),
        in_specs=[_tok_spec(D), _mod_spec(), _const_spec((4, D)), _const_spec((D, D)), _const_spec((1, D)),
                  _const_spec((2 * D, D)), _const_spec((2 * D, 128))],
        out_specs=[_tok_spec(D), _tokT_spec(2 * D)],
        out_shape=[jax.ShapeDtypeStruct((bsz, ttot, D), BF16),
                   jax.ShapeDtypeStruct((bsz, ttot // TB, 2 * D, TB), BF16)],
        compiler_params=_cparams(("arbitrary", "arbitrary")),
        name="na_inproj",
    )(xs, mods, ng, wk, b_qkv[D:2 * D].reshape(1, D), wt, bt)


def _na_bias_table(rpb, rows):
    assert rows >= 4 * NA_RB and rows % NA_RB == 0
    rr = np.arange(NA_RB)
    ib = np.arange(NA_KR)
    qc = np.arange(GRID_W)
    kc = np.arange(GRID_W)
    ws = np.clip(qc - NA_KW // 2, 0, GRID_W - NA_KW)
    valid_col = (kc[None, :] >= ws[:, None]) & (kc[None, :] < ws[:, None] + NA_KW)
    dcol = np.clip(kc[None, :] - qc[:, None], -(NA_KW - 1), NA_KW - 1) + NA_KW - 1
    tiles = jnp.where(jnp.asarray(valid_col)[None, None], rpb[:, :, dcol].astype(F32) * LOG2E, NEG)
    drows, valids = [], []
    for off, rs_minus_r in ((rr, np.full(NA_RB, -(NA_KH // 2))),
                            (np.full(NA_RB, NA_RB), -rr),
                            (np.zeros(NA_RB, np.int64), -(NA_KH // 2) - rr)):
        i = ib[None, :] - off[:, None]
        valids.append((i >= 0) & (i < NA_KH))
        drows.append(np.clip(rs_minus_r[:, None] + i + NA_KH - 1, 0, 2 * NA_KH - 2))
    drow = np.stack(drows)
    valid_row = np.stack(valids)
    t = tiles[:, drow]
    t = jnp.where(jnp.asarray(valid_row)[None, :, :, :, None, None], t, NEG)
    t = t.reshape(NA_HEADS // 2, 2, 3, NA_RB, NA_KR, GRID_W, GRID_W)
    t = jnp.transpose(t, (2, 0, 4, 6, 1, 3, 5))
    return t.reshape(3, NA_HEADS // 2, NA_KR * GRID_W, 2 * TB)


def _na_kernel(q_ref, kp_ref, kc_ref, kn_ref, kx_ref, vp_ref, vc_ref, vn_ref, vx_ref, bias_ref, o_ref):
    j = pl.program_id(2)
    row = lax.broadcasted_iota(jnp.int32, (2 * NA_DH, 1), 0)

    def heads(krefs, vrefs, nbias):
        def scores(pair):
            blk = slice(pair * 2 * NA_DH, (pair + 1) * 2 * NA_DH)
            q = q_ref[blk, :]
            zq = jnp.zeros_like(q)
            q2 = jnp.concatenate([jnp.where(row < NA_DH, q, zq), jnp.where(row < NA_DH, zq, q)], axis=1)
            kcat = jnp.concatenate([kr[:, blk] for kr in krefs], axis=0)
            return jnp.dot(kcat, q2, preferred_element_type=F32)

        def finish(pair, s):
            blk = slice(pair * 2 * NA_DH, (pair + 1) * 2 * NA_DH)
            ss = [s[i * TB:(i + 1) * TB] + bias_ref[pair, i * TB:(i + 1) * TB, :] if i < nbias
                  else s[i * TB:(i + 1) * TB] for i in range(len(krefs))]
            m = functools.reduce(jnp.maximum, [jnp.max(s, axis=0, keepdims=True) for s in ss])
            ps = [jnp.exp2(s - m) for s in ss]
            l = functools.reduce(jnp.add, [jnp.sum(p, axis=0, keepdims=True) for p in ps])
            vcat = jnp.concatenate([vr[blk, :] for vr in vrefs], axis=1)
            pcat = jnp.concatenate([p.astype(BF16) for p in ps], axis=0)
            o = jnp.dot(vcat, pcat, preferred_element_type=F32) * (1.0 / l)
            o_ref[blk, :] = jnp.where(row < NA_DH, o[:, :TB], o[:, TB:]).astype(BF16)

        s_next = scores(0)
        for pair in range(NA_PAIRS):
            s_cur = s_next
            if pair + 1 < NA_PAIRS:
                s_next = scores(pair + 1)
            finish(pair, s_cur)

    @pl.when(j == 0)
    def _():
        heads((kx_ref,), (vx_ref,), 0)

    @pl.when(j > 0)
    def _():
        heads((kp_ref, kc_ref, kn_ref, kx_ref), (vp_ref, vc_ref, vn_ref, vx_ref), 3)


def _na_attention(kt, pt, bias):
    bsz, ttot, _ = kt.shape
    nblk = ttot // TB
    hp = NA_HEADS // 2 // NA_PAIRS
    wb = 2 * NA_DH * NA_PAIRS

    def tok(fn):
        return pl.BlockSpec((None, TB, wb), lambda h, b, j: (b, fn(j), h))

    def tokT(row0, fn):
        return pl.BlockSpec((None, None, wb, TB), lambda h, b, j: (b, fn(j), row0 + h, 0))

    prev = lambda j: jnp.clip(j - 1, 1, nblk - 1)
    cur = lambda j: j
    nxt = lambda j: jnp.clip(j + 1, 1, nblk - 1)
    zero = lambda j: 0
    case = lambda j: jnp.where(j == 1, 1, jnp.where(j == nblk - 1, 2, 0))
    return pl.pallas_call(
        _na_kernel,
        grid=(hp, bsz, nblk),
        in_specs=[tokT(0, cur),
                  tok(prev), tok(cur), tok(nxt), tok(zero),
                  tokT(hp, prev), tokT(hp, cur), tokT(hp, nxt), tokT(hp, zero),
                  pl.BlockSpec((None, NA_PAIRS, NA_KR * GRID_W, 2 * TB), lambda h, b, j: (case(j), h, 0, 0))],
        out_specs=tokT(0, cur),
        out_shape=jax.ShapeDtypeStruct((bsz, nblk, D, TB), BF16),
        compiler_params=_cparams(("arbitrary", "arbitrary", "arbitrary")),
        name="na_attention",
    )(pt, kt, kt, kt, kt, pt, pt, pt, pt, bias)


def _na_layer(xs, mods, ng, w_qkv, b_qkv, rpb, w_o, b_o):
    rows = (xs.shape[1] - TB) // GRID_W
    kt, pt = _na_inproj(xs, mods, ng, w_qkv, b_qkv)
    att = _na_attention(kt, pt, _na_bias_table(rpb, rows))
    return ("na", att, w_o, b_o)


def _rope_tables(t_lat):
    pos = jnp.arange(t_lat)
    row = (pos // GRID_W).astype(F32)
    col = (pos % GRID_W).astype(F32)
    d_axis = ML_DK // 2
    inv = ROPE_BASE ** (-jnp.arange(0, d_axis, 2, dtype=F32) / d_axis)
    lane = np.arange(2 * ML_DK)
    d = lane % ML_DK
    use_col = (d // d_axis) == 1
    jdx = d % (d_axis // 2)
    first = (d % d_axis) < d_axis // 2
    ang_r = row[:, None] * inv[None, :]
    ang_c = col[:, None] * inv[None, :]
    ang = jnp.where(jnp.asarray(use_col)[None, :], ang_c[:, jdx], ang_r[:, jdx])
    cos = jnp.cos(ang)
    sin = jnp.sin(ang)
    ss = jnp.where(jnp.asarray(first)[None, :], -sin, sin)
    cos = jnp.concatenate([jnp.ones((TB, 2 * ML_DK), F32), cos], axis=0)
    ss = jnp.concatenate([jnp.zeros((TB, 2 * ML_DK), F32), ss], axis=0)
    return cos, ss


def _log_sigmoid(x):
    return jnp.minimum(x, 0.0) - jnp.log(1.0 + jnp.exp(-jnp.abs(x)))


def _lane_scan(x, op, ident, reverse):
    n = x.shape[1]
    lane = lax.broadcasted_iota(jnp.int32, (1, n), 1)
    d = 1
    while d < n:
        if reverse:
            sh = jnp.where(lane < n - d, pltpu.roll(x, n - d, 1), ident)
        else:
            sh = jnp.where(lane >= d, pltpu.roll(x, d, 1), ident)
        x = op(x, sh)
        d *= 2
    return x


def _ml_inproj_kernel(x_ref, mod_ref, g_ref, wk_ref, wt_ref, cos_ref, ss_ref, cost_ref, sst_ref,
                      wg_ref, bg_ref, k_ref, pt_ref, gr_ref, gc_ref):
    mod = mod_ref[...]
    g = g_ref[...]
    h = _premod(x_ref[...], g[0:1], mod[0:1], mod[1:2]).astype(BF16)
    half = ML_DK // 4

    nh = ML_HEADS
    bg = bg_ref[...]
    gr = lax.dot_general(wg_ref[...], h, DN_T, preferred_element_type=F32) + jnp.concatenate([bg, bg], axis=1)
    outs = []
    for dirn in range(2):
        ig = gr[2 * dirn * nh:(2 * dirn + 1) * nh]
        lf = _log_sigmoid(gr[(2 * dirn + 1) * nh:(2 * dirn + 2) * nh])
        b = _lane_scan(lf, jnp.add, 0.0, dirn == 1)
        u = ig - b
        outs += [u, _lane_scan(u, jnp.maximum, -jnp.inf, dirn == 1), b]
    gr_ref[...] = jnp.concatenate(outs, axis=0)
    ucols = jnp.concatenate([outs[0], outs[3], jnp.zeros((128 - 2 * nh, TB), F32)], axis=0)
    gc_ref[...] = ucols.T

    k = jnp.dot(h, wk_ref[...], preferred_element_type=F32)
    cos = cos_ref[...]
    ss = ss_ref[...]
    lane = lax.broadcasted_iota(jnp.int32, (1, 2 * ML_DK), 1)
    first_l = (lane % (2 * half)) < half
    for blk in range(ML_QK // 128):
        cols = slice(blk * 128, (blk + 1) * 128)
        xb = k[:, cols]
        sw = jnp.where(first_l, pltpu.roll(xb, 128 - half, 1), pltpu.roll(xb, half, 1))
        k_ref[:, cols] = ((xb * cos + sw * ss) * (ML_DK ** -0.5)).astype(BF16)

    pt = lax.dot_general(wt_ref[...], h, DN_T, preferred_element_type=F32)
    pt_ref[:2 * ML_INNER, :] = pt[:2 * ML_INNER].astype(BF16)
    cost = cost_ref[...]
    sst = sst_ref[...]
    row = lax.broadcasted_iota(jnp.int32, (2 * ML_DK, 1), 0)
    first_r = (row % (2 * half)) < half
    for blk in range(ML_QK // 128):
        rows = slice(2 * ML_INNER + blk * 128, 2 * ML_INNER + (blk + 1) * 128)
        xb = pt[rows]
        sw = jnp.where(first_r, pltpu.roll(xb, 128 - half, 0), pltpu.roll(xb, half, 0))
        pt_ref[rows, :] = (xb * cost + sw * sst).astype(BF16)


def _ml_inproj(xs, mods, ng, w_in, w_gate, b_gate):
    bsz, ttot, _ = xs.shape
    ngate = 4 * ML_HEADS
    cos, ss = _rope_tables(ttot - TB)
    wk = w_in[:, ML_QK:2 * ML_QK]
    wt = jnp.concatenate([w_in[:, 2 * ML_QK + ML_INNER:], w_in[:, 2 * ML_QK:2 * ML_QK + ML_INNER],
                          w_in[:, :ML_QK]], axis=1).T
    wg = jnp.concatenate([w_gate[0], w_gate[1]], axis=1).astype(BF16).T
    bg = jnp.broadcast_to(b_gate.reshape(ngate, 1).astype(F32), (ngate, 128))
    nt = wt.shape[0]
    tab = pl.BlockSpec((TB, 2 * ML_DK), lambda b, j: (j, 0))
    tabT = pl.BlockSpec((2 * ML_DK, TB), lambda b, j: (0, j))
    return pl.pallas_call(
        _ml_inproj_kernel,
        grid=(bsz, ttot // TB),
        in_specs=[_tok_spec(D), _mod_spec(), _const_spec((4, D)), _const_spec(wk.shape), _const_spec(wt.shape),
                  tab, tab, tabT, tabT, _const_spec((ngate, D)), _const_spec((ngate, 128))],
        out_specs=[_tok_spec(ML_QK), _tokT_spec(nt), _tokT_spec(6 * ML_HEADS), _tok_spec(128)],
        out_shape=[jax.ShapeDtypeStruct((bsz, ttot, ML_QK), BF16),
                   jax.ShapeDtypeStruct((bsz, ttot // TB, nt, TB), BF16),
                   jax.ShapeDtypeStruct((bsz, ttot // TB, 6 * ML_HEADS, TB), F32),
                   jax.ShapeDtypeStruct((bsz, ttot, 128), F32)],
        compiler_params=_cparams(("arbitrary", "arbitrary")),
        name="ml_inproj",
    )(xs, mods, ng, wk, wt, cos, ss, cos.T, ss.T, wg, bg)


def _ml_scan_kernel(kf_ref, vf_ref, qf_ref, grf_ref, gcf_ref, kb_ref, vb_ref, qb_ref, grb_ref, gcb_ref,
                    hf_ref, hb_ref, s_scr, m_scr):
    step = pl.program_id(1)

    @pl.when(step == 0)
    def _():
        s_scr[...] = jnp.zeros_like(s_scr)
        m_scr[...] = jnp.zeros_like(m_scr)

    nh = ML_HEADS
    lane = lax.broadcasted_iota(jnp.int32, (1, 2 * ML_DK), 1)
    r_i = lax.broadcasted_iota(jnp.int32, (TB, TB), 0)
    c_i = lax.broadcasted_iota(jnp.int32, (TB, TB), 1)
    ones_rows = jnp.ones((ML_ONES, TB), BF16)

    dirs = ((kf_ref, vf_ref, qf_ref, grf_ref, gcf_ref, hf_ref, r_i <= c_i, TB - 1),
            (kb_ref, vb_ref, qb_ref, grb_ref, gcb_ref, hb_ref, r_i >= c_i, 0))
    stats = []
    for dirn, (_, _, _, gr_ref, _, _, _, last) in enumerate(dirs):
        u = gr_ref[0:nh, :]
        pm = gr_ref[nh:2 * nh, :]
        b = gr_ref[2 * nh:3 * nh, :]
        m_st = m_scr[dirn][:, 0:1]
        mrow = jnp.maximum(m_st, pm)
        m_last = mrow[:, last:last + 1]
        stats.append(dict(mrow=mrow, inter=jnp.exp(m_st - mrow), em=jnp.exp(-(b + mrow)),
                          wrow=jnp.exp(u - m_last), decay=jnp.exp(m_st - m_last)))
        m_scr[dirn] = jnp.broadcast_to(b[:, last:last + 1] + m_last, (nh, 128))

    def stage_a(i):
        dirn, hd = divmod(i, nh)
        k_ref, v_ref, q_ref = dirs[dirn][:3]
        pair, a = divmod(hd, 2)
        kp = k_ref[:, pair * 128:(pair + 1) * 128]
        sel = (lane >= a * ML_DK) & (lane < (a + 1) * ML_DK)
        km = jnp.where(sel, kp, jnp.zeros_like(kp))
        qt = q_ref[pair * 128:(pair + 1) * 128, :]
        vaug = jnp.concatenate([v_ref[hd * ML_DV:(hd + 1) * ML_DV, :], ones_rows], axis=0)
        st = s_scr[i]
        s_t = jnp.dot(km, qt, preferred_element_type=F32)
        r2 = jnp.dot(st.astype(BF16), qt, preferred_element_type=F32)
        return km, vaug, st, s_t, r2

    def stage_b(i, km, vaug, st, s_t, r2):
        dirn, hd = divmod(i, nh)
        gc_ref, out_ref, tri = dirs[dirn][4:7]
        sd = stats[dirn]
        u_col = gc_ref[:, i:i + 1]
        d_t = jnp.where(tri, jnp.exp(u_col - sd["mrow"][hd:hd + 1, :]), 0.0)
        a_t = (s_t * d_t).astype(BF16)
        res = jnp.dot(vaug, a_t, preferred_element_type=F32) + sd["inter"][hd:hd + 1, :] * r2
        den = jnp.maximum(jnp.abs(res[ML_DV:ML_DV + 1, :]), sd["em"][hd:hd + 1, :])
        out_ref[hd * ML_DV:(hd + 1) * ML_DV, :] = res[:ML_DV] * (1.0 / den)
        vw = (vaug.astype(F32) * sd["wrow"][hd:hd + 1, :]).astype(BF16)
        s_scr[i] = sd["decay"][hd:hd + 1, :] * st + jnp.dot(vw, km, preferred_element_type=F32)

    nxt = stage_a(0)
    for i in range(2 * nh):
        cur = nxt
        if i + 1 < 2 * nh:
            nxt = stage_a(i + 1)
        stage_b(i, *cur)


def _ml_scan(kt, pt, grow, gcol):
    bsz, ttot, _ = kt.shape
    nblk = ttot // TB
    fwd = lambda i: i
    bwd = lambda i: jnp.where(i == 0, 0, nblk - i)
    vrow = ML_INNER // ML_INNER
    qrow = 2 * ML_INNER // ML_QK

    def specs(fn, dirn):
        return [pl.BlockSpec((None, TB, ML_QK), lambda b, i: (b, fn(i), 0)),
                pl.BlockSpec((None, None, ML_INNER, TB), lambda b, i: (b, fn(i), vrow, 0)),
                pl.BlockSpec((None, None, ML_QK, TB), lambda b, i: (b, fn(i), qrow, 0)),
                pl.BlockSpec((None, None, 3 * ML_HEADS, TB), lambda b, i: (b, fn(i), dirn, 0)),
                pl.BlockSpec((None, TB, 128), lambda b, i: (b, fn(i), 0))]

    return pl.pallas_call(
        _ml_scan_kernel,
        grid=(bsz, nblk),
        in_specs=specs(fwd, 0) + specs(bwd, 1),
        out_specs=[pl.BlockSpec((None, None, ML_INNER, TB), lambda b, i: (b, fwd(i), 0, 0)),
                   pl.BlockSpec((None, None, ML_INNER, TB), lambda b, i: (b, bwd(i), 0, 0))],
        out_shape=[jax.ShapeDtypeStruct((bsz, nblk, ML_INNER, TB), F32)] * 2,
        scratch_shapes=[pltpu.VMEM((2 * ML_HEADS, ML_DV + ML_ONES, 2 * ML_DK), F32),
                        pltpu.VMEM((2, ML_HEADS, 128), F32)],
        compiler_params=_cparams(("arbitrary", "arbitrary")),
        name="ml_scan",
    )(kt, pt, pt, grow, gcol, kt, pt, pt, grow, gcol)


def _ml_layer(xs, mods, ng, w_in, w_gate, b_gate, ml_g, w_out):
    kt, pt, grow, gcol = _ml_inproj(xs, mods, ng, w_in, w_gate, b_gate)
    hf, hb = _ml_scan(kt, pt, grow, gcol)
    return ("ml", hf, hb, pt, ml_g, w_out)


def kernel(x, c, ctx, c_ctx, ada_w, ada_b, norm_g, ffn_w1, ffn_w2, gm_w_in, gm_b_in, gm_ln_g, gm_ws, gm_bs, gm_w_out, na_w_qkv, na_b_qkv, na_rpb, na_w_o, na_b_o, ml_w_in, ml_w_gate, ml_b_gate, ml_norm_g, ml_w_out):
    depth = ada_w.shape[0]
    assert ctx.shape[1] == TB and x.shape[1] % TB == 0 and x.shape[0] <= CTX_ROW
    mods_all = _mods_all(c, c_ctx, ada_w, ada_b)
    xs = (ctx, x) if depth > 1 else jnp.concatenate([ctx, x], axis=1)
    for i in range(depth):
        kind, j = i % 3, i // 3
        mods = mods_all[i]
        ng = norm_g[i]
        last = i == depth - 1
        mixer = None
        if kind == 0:
            xs = _gmlp_layer(xs, mods, ng, gm_w_in[j].astype(BF16), gm_b_in[j], gm_ln_g[j],
                             gm_ws[j].astype(BF16), gm_bs[j], gm_w_out[j].astype(BF16), last)
        elif kind == 1:
            mixer = _na_layer(xs, mods, ng, na_w_qkv[j].astype(BF16), na_b_qkv[j], na_rpb[j],
                              na_w_o[j].astype(BF16), na_b_o[j])
        else:
            mixer = _ml_layer(xs, mods, ng, ml_w_in[j].astype(BF16), ml_w_gate[j], ml_b_gate[j],
                              ml_norm_g[j], ml_w_out[j].astype(BF16))
        has_ctx = xs.shape[1] != x.shape[1]
        xs = _ffn_layer(xs, mods, ng, ffn_w1[i].astype(BF16), ffn_w2[i].astype(BF16), has_ctx, has_ctx and last,
                        mixer)
    return xs
```

```python
import functools

import numpy as np
import jax
import jax.numpy as jnp
from jax import lax
from jax.experimental import pallas as pl
from jax.experimental.pallas import tpu as pltpu

F32 = jnp.float32
BF16 = jnp.bfloat16

D = 1024
TB = 256
GRID_W = 64
EPS = 1e-6
CTX_ROW = 8
MOD_ROWS = 16
GM_HALF = 3 * D
GM_CHUNK = 128
GM_GROUPS = 8
GM_GW = GM_HALF // GM_GROUPS
NA_HEADS = 16
NA_DH = 64
NA_KH = 8
NA_KW = 16
NA_RB = TB // GRID_W
NA_KR = 3 * NA_RB
NEG = -1e30
LOG2E = float(np.log2(np.e))
NA_QSCALE = NA_DH ** -0.5 * LOG2E
NA_PAIRS = 4
ML_HEADS = 8
ML_DK = 64
ML_DV = 128
ML_QK = ML_HEADS * ML_DK
ML_INNER = ML_HEADS * ML_DV
ML_ONES = 16
ROPE_BASE = 10000.0

VMEM_LIMIT = 56 * 1024 * 1024

DN_T = (((1,), (1,)), ((), ()))
DN_0 = (((0,), (0,)), ((), ()))


def _cparams(sem, flags=None):
    return pltpu.CompilerParams(dimension_semantics=sem, vmem_limit_bytes=VMEM_LIMIT, flags=flags)


def _rms(x, g):
    ms = jnp.mean(x * x, axis=-1, keepdims=True)
    return x * lax.rsqrt(ms + EPS) * g


def _premod(x, g, shift, scale):
    return _rms(x, g) * (1.0 + scale) + shift


def _const_spec(shape):
    nd = len(shape)
    return pl.BlockSpec(shape, lambda *_: (0,) * nd, pipeline_mode=pl.Buffered(1))


def _tok_spec(width, off=0, col=0):
    return pl.BlockSpec((None, TB, width), lambda b, j: (b, j + off, col))


def _tokT_spec(height):
    return pl.BlockSpec((None, None, height, TB), lambda b, j: (b, j, 0, 0))


def _mod_spec(ctx_first=True):
    if ctx_first:
        return pl.BlockSpec((None, 6, D), lambda b, j: (jnp.where(j == 0, CTX_ROW, b), 0, 0))
    return pl.BlockSpec((None, 6, D), lambda b, j: (b, 0, 0))


def _mod_kernel(c_ref, w_ref, b_ref, o_ref):
    c = c_ref[...]
    a = (c * jax.nn.sigmoid(c)).astype(BF16)
    o_ref[...] = jnp.dot(a, w_ref[...].astype(BF16), preferred_element_type=F32) + b_ref[...]


def _mods_all(c, c_ctx, ada_w, ada_b):
    depth = ada_w.shape[0]
    n = ada_w.shape[2]
    tn = 1536
    cv = jnp.zeros((MOD_ROWS, D), F32).at[:c.shape[0]].set(c).at[CTX_ROW].set(c_ctx)
    out = pl.pallas_call(
        _mod_kernel,
        grid=(depth, n // tn),
        in_specs=[pl.BlockSpec((MOD_ROWS, D), lambda l, k: (0, 0)),
                  pl.BlockSpec((None, D, tn), lambda l, k: (l, 0, k)),
                  pl.BlockSpec((None, 1, tn), lambda l, k: (l, 0, k))],
        out_specs=pl.BlockSpec((None, MOD_ROWS, tn), lambda l, k: (l, 0, k)),
        out_shape=jax.ShapeDtypeStruct((depth, MOD_ROWS, n), F32),
        compiler_params=_cparams(("arbitrary", "arbitrary")),
        name="adaln_mod",
    )(cv, ada_w, ada_b.reshape(depth, 1, n))
    return out.reshape(depth, MOD_ROWS, 6, D)


def _software_pipeline(n_items, stages):
    carry = [None] * n_items
    for t in range(n_items + len(stages) - 1):
        for k, stage in enumerate(stages):
            if 0 <= t - k < n_items:
                carry[t - k] = stage(t - k, carry[t - k])


def _sub_plan(bsz, nblk, off, over_batch):
    if over_batch:
        n = 2 if bsz % 2 == 0 else 1
        return (n, (bsz // n, nblk), lambda s: (lambda b, j: n * b + s), lambda s: (lambda b, j: off + j),
                pl.BlockSpec((n, TB, D), lambda b, j: (b, j, 0)))
    n = 3 if nblk % 3 == 0 else (2 if nblk % 2 == 0 else 1)
    return (n, (bsz, nblk // n), lambda s: (lambda b, j: b), lambda s: (lambda b, j: off + n * j + s),
            pl.BlockSpec((None, n * TB, D), lambda b, j: (b, j, 0)))


def _store_sub(o_ref, s, over_batch, val):
    if over_batch:
        o_ref[s] = val
    else:
        o_ref[s * TB:(s + 1) * TB, :] = val


def _ffn_kernel(*refs, n_sub, over_batch, mixer):
    per = {None: 2, "na": 3, "ml": 5}[mixer]
    subs = [refs[s * per:(s + 1) * per] for s in range(n_sub)]
    consts, o_ref = refs[n_sub * per:-1], refs[-1]
    g = consts[0][...]
    w1_ref, w2_ref = consts[-2:]

    def mixer_proj(s, _):
        if mixer == "na":
            wo_ref, bo_ref = consts[1:3]
            return lax.dot_general(subs[s][2][...], wo_ref[...], DN_0, preferred_element_type=F32) + bo_ref[...]
        hf_ref, hb_ref, og_ref = subs[s][2:5]
        mlg_ref, wo_ref = consts[1:3]
        hs = hf_ref[...] + hb_ref[...]
        parts = []
        for hd in range(ML_HEADS):
            xh = hs[hd * ML_DV:(hd + 1) * ML_DV]
            parts.append(xh * lax.rsqrt(jnp.mean(xh * xh, axis=0, keepdims=True) + EPS))
        mlg = mlg_ref[...]
        hh = jnp.concatenate(parts, axis=0) * jnp.concatenate([mlg, mlg], axis=1)
        yin = (hh * jax.nn.sigmoid(og_ref[...].astype(F32))).astype(BF16)
        return lax.dot_general(yin, wo_ref[...], DN_0, preferred_element_type=F32)

    def up(s, ymix):
        x = subs[s][0][...]
        mod = subs[s][1][...]
        if mixer is not None:
            x = x + mod[2:3] * _rms(ymix, g[1:2])
        h = _premod(x, g[2:3], mod[3:4], mod[4:5]).astype(BF16)
        return x, jnp.dot(h, w1_ref[...], preferred_element_type=F32)

    def down(s, c):
        a = jnp.square(jnp.maximum(c[1], 0.0)).astype(BF16)
        return c[0], jnp.dot(a, w2_ref[...], preferred_element_type=F32)

    def finish(s, c):
        mod = subs[s][1][...]
        _store_sub(o_ref, s, over_batch, c[0] + mod[5:6] * _rms(c[1], g[3:4]))

    _software_pipeline(n_sub, ([] if mixer is None else [mixer_proj]) + [up, down, finish])


def _ffn_layer(xs, mods, ng, w1, w2, has_ctx, drop_ctx, mixer=None):
    bsz, ttot, _ = xs.shape
    off = 1 if drop_ctx else 0
    nblk = ttot // TB - off
    over_batch = mixer is not None and mixer[0] == "ml"
    n_sub, grid, b_of, j_of, out_spec = _sub_plan(bsz, nblk, off, over_batch)
    ctx_first = has_ctx and not drop_ctx

    def mod_spec(s):
        bi, ji = b_of(s), j_of(s)
        row = (lambda b, j: jnp.where(ji(b, j) == 0, CTX_ROW, bi(b, j))) if ctx_first else bi
        return pl.BlockSpec((None, 6, D), lambda b, j: (row(b, j), 0, 0))

    def tok(s):
        bi, ji = b_of(s), j_of(s)
        return pl.BlockSpec((None, TB, D), lambda b, j: (bi(b, j), ji(b, j), 0))

    def tokT(s, height):
        bi, ji = b_of(s), j_of(s)
        return pl.BlockSpec((None, None, height, TB), lambda b, j: (bi(b, j), ji(b, j), 0, 0))

    kind, per_sub, consts, const_specs = None, (), (), []
    if mixer is not None and mixer[0] == "na":
        kind, yt, w_o, b_o = mixer
        per_sub, consts = ((yt, D),), (w_o, b_o.reshape(1, D))
        const_specs = [_const_spec(w_o.shape), _const_spec((1, D))]
    elif mixer is not None:
        kind, hf, hb, pt, ml_g, w_o = mixer
        mlg = jnp.broadcast_to(ml_g.astype(F32)[:, None], (ML_INNER, 128))
        per_sub, consts = ((hf, ML_INNER), (hb, ML_INNER), (pt, ML_INNER)), (mlg, w_o)
        const_specs = [_const_spec((ML_INNER, 128)), _const_spec(w_o.shape)]
    args, specs = [], []
    for s in range(n_sub):
        args += [xs, mods] + [a for a, _ in per_sub]
        specs += [tok(s), mod_spec(s)] + [tokT(s, hgt) for _, hgt in per_sub]
    return pl.pallas_call(
        functools.partial(_ffn_kernel, n_sub=n_sub, over_batch=over_batch, mixer=kind),
        grid=grid,
        in_specs=specs + [_const_spec((4, D))] + const_specs + [_const_spec(w1.shape), _const_spec(w2.shape)],
        out_specs=out_spec,
        out_shape=jax.ShapeDtypeStruct((bsz, nblk * TB, D), F32),
        compiler_params=_cparams(("arbitrary", "arbitrary")),
        name="ffn" if mixer is None else kind + "_ffn",
    )(*args, ng, *consts, w1, w2)


def _gelu_tanh(x):
    return 0.5 * x * (1.0 + jnp.tanh(np.float32(np.sqrt(2.0 / np.pi)) * (x + 0.044715 * (x * x * x))))


def _gmlp_kernel(*refs, n_sub, split_input):
    per = 3 if split_input else 2
    subs = [refs[s * per:(s + 1) * per] for s in range(n_sub)]
    g_ref, win_ref, bin_ref, lng_ref, ws_ref, bst_ref, wout_ref, o_ref, v_scr, t_scr = refs[n_sub * per:]
    g = g_ref[...]

    def v_proj(s, _):
        if split_input:
            x = jnp.where(pl.program_id(1) == 0, subs[s][0][...], subs[s][1][...])
        else:
            x = subs[s][0][...]
        mod = subs[s][-1][...]
        h = _premod(x, g[0:1], mod[0:1], mod[1:2]).astype(BF16)
        return x, h, jnp.dot(h, win_ref[:, GM_HALF:], preferred_element_type=F32) + bin_ref[:, GM_HALF:]

    def u_proj(s, c):
        x, h, zv = c
        v = _gelu_tanh(zv)
        vc = v - jnp.mean(v, axis=-1, keepdims=True)
        v = vc * lax.rsqrt(jnp.mean(vc * vc, axis=-1, keepdims=True) + EPS) * lng_ref[...]
        v_scr[s] = v.astype(BF16)
        return x, jnp.dot(h, win_ref[:, :GM_HALF], preferred_element_type=F32) + bin_ref[:, :GM_HALF]

    def mix(s, c):
        x, zu = c
        u = _gelu_tanh(zu)
        bst = bst_ref[...]
        for n in range(TB // GM_CHUNK):
            rows = slice(n * GM_CHUNK, (n + 1) * GM_CHUNK)
            for gi in range(GM_GROUPS):
                cols = slice(gi * GM_GW, (gi + 1) * GM_GW)
                sp = jnp.dot(ws_ref[gi], v_scr[s, rows, cols], preferred_element_type=F32) + bst[:, gi:gi + 1]
                t_scr[s, rows, cols] = (u[rows, cols] * sp).astype(BF16)
        return x, jnp.dot(t_scr[s], wout_ref[...], preferred_element_type=F32)

    def finish(s, c):
        mod = subs[s][-1][...]
        o_ref[s] = c[0] + mod[2:3] * _rms(c[1], g[1:2])

    _software_pipeline(n_sub, [v_proj, u_proj, mix, finish])


def _gmlp_layer(xs, mods, ng, w_in, b_in, ln_g, ws, bs, w_out, drop_ctx):
    split = isinstance(xs, tuple)
    if split:
        assert not drop_ctx
        bsz, ttot = xs[1].shape[0], xs[0].shape[1] + xs[1].shape[1]
    else:
        bsz, ttot, _ = xs.shape
        xs = (xs,)
    off = 1 if drop_ctx else 0
    nblk = ttot // TB - off
    n_sub, grid, b_of, j_of, out_spec = _sub_plan(bsz, nblk, off, True)
    args, specs = [], []
    for s in range(n_sub):
        bi, ji = b_of(s), j_of(s)
        if split:
            specs += [pl.BlockSpec((None, TB, D), lambda b, j, bi=bi: (bi(b, j), 0, 0)),
                      pl.BlockSpec((None, TB, D), lambda b, j, bi=bi: (bi(b, j), jnp.maximum(j - 1, 0), 0))]
        else:
            specs += [pl.BlockSpec((None, TB, D), lambda b, j, bi=bi, ji=ji: (bi(b, j), ji(b, j), 0))]
        row = (lambda b, j, bi=bi: bi(b, j)) if drop_ctx else (lambda b, j, bi=bi: jnp.where(j == 0, CTX_ROW, bi(b, j)))
        specs += [pl.BlockSpec((None, 6, D), lambda b, j, row=row: (row(b, j), 0, 0))]
        args += list(xs) + [mods]
    return pl.pallas_call(
        functools.partial(_gmlp_kernel, n_sub=n_sub, split_input=split),
        grid=grid,
        in_specs=specs + [_const_spec((4, D)),
                          _const_spec(w_in.shape), _const_spec((1, 2 * GM_HALF)), _const_spec((1, GM_HALF)),
                          _const_spec(ws.shape), _const_spec((GM_CHUNK, GM_GROUPS)), _const_spec(w_out.shape)],
        out_specs=out_spec,
        out_shape=jax.ShapeDtypeStruct((bsz, nblk * TB, D), F32),
        scratch_shapes=[pltpu.VMEM((n_sub, TB, GM_HALF), BF16), pltpu.VMEM((n_sub, TB, GM_HALF), BF16)],
        compiler_params=_cparams(("arbitrary", "arbitrary")),
        name="gmlp",
    )(*args, ng, w_in, b_in.reshape(1, -1), ln_g.reshape(1, -1), ws, bs.T, w_out)


def _na_inproj_kernel(x_ref, mod_ref, g_ref, wk_ref, bk_ref, wt_ref, bt_ref, k_ref, pt_ref):
    mod = mod_ref[...]
    g = g_ref[...]
    h = _premod(x_ref[...], g[0:1], mod[0:1], mod[1:2]).astype(BF16)
    k_ref[...] = (jnp.dot(h, wk_ref[...], preferred_element_type=F32) + bk_ref[...]).astype(BF16)
    bt = bt_ref[...]
    pt = lax.dot_general(wt_ref[...], h, DN_T, preferred_element_type=F32) + jnp.concatenate([bt, bt], axis=1)
    pt_ref[:D, :] = (pt[:D] * NA_QSCALE).astype(BF16)
    pt_ref[D:, :] = pt[D:].astype(BF16)


def _na_inproj(xs, mods, ng, w_qkv, b_qkv):
    bsz, ttot, _ = xs.shape
    wk = w_qkv[:, D:2 * D]
    wt = jnp.concatenate([w_qkv[:, :D], w_qkv[:, 2 * D:]], axis=1).T
    bt = jnp.concatenate([b_qkv[:D], b_qkv[2 * D:]]).astype(F32)
    bt = jnp.broadcast_to(bt[:, None], (2 * D, 128))
    return pl.pallas_call(
        _na_inproj_kernel,
        grid=(bsz, ttot // TB),
        in_specs=[_tok_spec(D), _mod_spec(), _const_spec((4, D)), _const_spec((D, D)), _const_spec((1, D)),
                  _const_spec((2 * D, D)), _const_spec((2 * D, 128))],
        out_specs=[pl.BlockSpec((None, None, TB, D), lambda b, j: (b, j, 0, 0)), _tokT_spec(2 * D)],
        out_shape=[jax.ShapeDtypeStruct((bsz, ttot // TB, TB, D), BF16),
                   jax.ShapeDtypeStruct((bsz, ttot // TB, 2 * D, TB), BF16)],
        compiler_params=_cparams(("arbitrary", "arbitrary")),
        name="na_inproj",
    )(xs, mods, ng, wk, b_qkv[D:2 * D].reshape(1, D), wt, bt)


def _na_bias_table(rpb, rows):
    assert rows >= 4 * NA_RB and rows % NA_RB == 0
    rr = np.arange(NA_RB)
    ib = np.arange(NA_KR)
    qc = np.arange(GRID_W)
    kc = np.arange(GRID_W)
    ws = np.clip(qc - NA_KW // 2, 0, GRID_W - NA_KW)
    valid_col = (kc[None, :] >= ws[:, None]) & (kc[None, :] < ws[:, None] + NA_KW)
    dcol = np.clip(kc[None, :] - qc[:, None], -(NA_KW - 1), NA_KW - 1) + NA_KW - 1
    tiles = jnp.where(jnp.asarray(valid_col)[None, None], rpb[:, :, dcol].astype(F32) * LOG2E, NEG)
    drows, valids = [], []
    for off, rs_minus_r in ((rr, np.full(NA_RB, -(NA_KH // 2))),
                            (np.zeros(NA_RB, np.int64), -rr),
                            (np.full(NA_RB, NA_RB), -(NA_KH // 2) - rr)):
        i = ib[None, :] - off[:, None]
        valids.append((i >= 0) & (i < NA_KH))
        drows.append(np.clip(rs_minus_r[:, None] + i + NA_KH - 1, 0, 2 * NA_KH - 2))
    drow = np.stack(drows)
    valid_row = np.stack(valids)
    t = tiles[:, drow]
    t = jnp.where(jnp.asarray(valid_row)[None, :, :, :, None, None], t, NEG)
    t = t.reshape(NA_HEADS // 2, 2, 3, NA_RB, NA_KR, GRID_W, GRID_W)
    t = jnp.transpose(t, (2, 0, 4, 6, 1, 3, 5))
    return t.reshape(3, NA_HEADS // 2, NA_KR * GRID_W, 2 * TB)


def _na_kernel(q_ref, kw_ref, kx_ref, vw_ref, vx_ref, bias_ref, o_ref):
    j = pl.program_id(2)
    row = lax.broadcasted_iota(jnp.int32, (2 * NA_DH, 1), 0)

    def heads(nwin):
        def scores(pair):
            blk = slice(pair * 2 * NA_DH, (pair + 1) * 2 * NA_DH)
            q = q_ref[blk, :]
            zq = jnp.zeros_like(q)
            q2 = jnp.concatenate([jnp.where(row < NA_DH, q, zq), jnp.where(row < NA_DH, zq, q)], axis=1)
            kcat = jnp.concatenate([kw_ref[0, i, :, blk] for i in range(nwin)] + [kx_ref[:, blk]], axis=0)
            return jnp.dot(kcat, q2, preferred_element_type=F32)

        def finish(pair, s):
            blk = slice(pair * 2 * NA_DH, (pair + 1) * 2 * NA_DH)
            ss = [s[i * TB:(i + 1) * TB] + bias_ref[pair, i * TB:(i + 1) * TB, :] if i < nwin
                  else s[i * TB:(i + 1) * TB] for i in range(nwin + 1)]
            m = functools.reduce(jnp.maximum, [jnp.max(s, axis=0, keepdims=True) for s in ss])
            ps = [jnp.exp2(s - m) for s in ss]
            l = functools.reduce(jnp.add, [jnp.sum(p, axis=0, keepdims=True) for p in ps])
            vcat = jnp.concatenate([vw_ref[0, i, blk, :] for i in range(nwin)] + [vx_ref[blk, :]], axis=1)
            pcat = jnp.concatenate([p.astype(BF16) for p in ps], axis=0)
            o = jnp.dot(vcat, pcat, preferred_element_type=F32) * (1.0 / l)
            o_ref[blk, :] = jnp.where(row < NA_DH, o[:, :TB], o[:, TB:]).astype(BF16)

        s_next = scores(0)
        for pair in range(NA_PAIRS):
            s_cur = s_next
            if pair + 1 < NA_PAIRS:
                s_next = scores(pair + 1)
            finish(pair, s_cur)

    @pl.when(j == 0)
    def _():
        heads(0)

    @pl.when(j > 0)
    def _():
        heads(3)


def _na_attention(kt, pt, bias):
    bsz, nblk = kt.shape[:2]
    hp = NA_HEADS // 2 // NA_PAIRS
    wb = 2 * NA_DH * NA_PAIRS
    win = lambda j: jnp.clip(j - 1, 1, nblk - 3)
    case = lambda j: jnp.where(j == 1, 1, jnp.where(j == nblk - 1, 2, 0))
    qo_spec = pl.BlockSpec((None, None, wb, TB), lambda h, b, j: (b, j, h, 0))
    return pl.pallas_call(
        _na_kernel,
        grid=(hp, bsz, nblk),
        in_specs=[qo_spec,
                  pl.BlockSpec((pl.Element(1), pl.Element(3), pl.Element(TB), pl.Element(wb)),
                               lambda h, b, j: (b, win(j), 0, h * wb)),
                  pl.BlockSpec((None, None, TB, wb), lambda h, b, j: (b, 0, 0, h)),
                  pl.BlockSpec((pl.Element(1), pl.Element(3), pl.Element(wb), pl.Element(TB)),
                               lambda h, b, j: (b, win(j), (hp + h) * wb, 0)),
                  pl.BlockSpec((None, None, wb, TB), lambda h, b, j: (b, 0, hp + h, 0)),
                  pl.BlockSpec((None, NA_PAIRS, NA_KR * GRID_W, 2 * TB), lambda h, b, j: (case(j), h, 0, 0))],
        out_specs=qo_spec,
        out_shape=jax.ShapeDtypeStruct((bsz, nblk, D, TB), BF16),
        compiler_params=_cparams(("arbitrary", "arbitrary", "arbitrary")),
        name="na_attention",
    )(pt, kt, kt, pt, pt, bias)


def _na_layer(xs, mods, ng, w_qkv, b_qkv, rpb, w_o, b_o):
    rows = (xs.shape[1] - TB) // GRID_W
    kt, pt = _na_inproj(xs, mods, ng, w_qkv, b_qkv)
    att = _na_attention(kt, pt, _na_bias_table(rpb, rows))
    return ("na", att, w_o, b_o)


def _rope_tables(t_lat):
    pos = jnp.arange(t_lat)
    row = (pos // GRID_W).astype(F32)
    col = (pos % GRID_W).astype(F32)
    d_axis = ML_DK // 2
    inv = ROPE_BASE ** (-jnp.arange(0, d_axis, 2, dtype=F32) / d_axis)
    lane = np.arange(2 * ML_DK)
    d = lane % ML_DK
    use_col = (d // d_axis) == 1
    jdx = d % (d_axis // 2)
    first = (d % d_axis) < d_axis // 2
    ang_r = row[:, None] * inv[None, :]
    ang_c = col[:, None] * inv[None, :]
    ang = jnp.where(jnp.asarray(use_col)[None, :], ang_c[:, jdx], ang_r[:, jdx])
    cos = jnp.cos(ang)
    sin = jnp.sin(ang)
    ss = jnp.where(jnp.asarray(first)[None, :], -sin, sin)
    cos = jnp.concatenate([jnp.ones((TB, 2 * ML_DK), F32), cos], axis=0)
    ss = jnp.concatenate([jnp.zeros((TB, 2 * ML_DK), F32), ss], axis=0)
    return cos, ss


def _log_sigmoid(x):
    return jnp.minimum(x, 0.0) - jnp.log(1.0 + jnp.exp(-jnp.abs(x)))


def _lane_scan(x, op, ident, reverse):
    n = x.shape[1]
    lane = lax.broadcasted_iota(jnp.int32, (1, n), 1)
    d = 1
    while d < n:
        if reverse:
            sh = jnp.where(lane < n - d, pltpu.roll(x, n - d, 1), ident)
        else:
            sh = jnp.where(lane >= d, pltpu.roll(x, d, 1), ident)
        x = op(x, sh)
        d *= 2
    return x


def _ml_inproj_kernel(x_ref, mod_ref, g_ref, wk_ref, wt_ref, cos_ref, ss_ref, cost_ref, sst_ref,
                      wg_ref, bg_ref, k_ref, pt_ref, gr_ref, gc_ref):
    mod = mod_ref[...]
    g = g_ref[...]
    h = _premod(x_ref[...], g[0:1], mod[0:1], mod[1:2]).astype(BF16)
    half = ML_DK // 4

    nh = ML_HEADS
    bg = bg_ref[...]
    gr = lax.dot_general(wg_ref[...], h, DN_T, preferred_element_type=F32) + jnp.concatenate([bg, bg], axis=1)
    outs = []
    for dirn in range(2):
        ig = gr[2 * dirn * nh:(2 * dirn + 1) * nh]
        lf = _log_sigmoid(gr[(2 * dirn + 1) * nh:(2 * dirn + 2) * nh])
        b = _lane_scan(lf, jnp.add, 0.0, dirn == 1)
        u = ig - b
        outs += [u, _lane_scan(u, jnp.maximum, -jnp.inf, dirn == 1), b]
    gr_ref[...] = jnp.concatenate(outs, axis=0)
    ucols = jnp.concatenate([outs[0], outs[3], jnp.zeros((128 - 2 * nh, TB), F32)], axis=0)
    gc_ref[...] = ucols.T

    k = jnp.dot(h, wk_ref[...], preferred_element_type=F32)
    cos = cos_ref[...]
    ss = ss_ref[...]
    lane = lax.broadcasted_iota(jnp.int32, (1, 2 * ML_DK), 1)
    first_l = (lane % (2 * half)) < half
    for blk in range(ML_QK // 128):
        cols = slice(blk * 128, (blk + 1) * 128)
        xb = k[:, cols]
        sw = jnp.where(first_l, pltpu.roll(xb, 128 - half, 1), pltpu.roll(xb, half, 1))
        k_ref[:, cols] = ((xb * cos + sw * ss) * (ML_DK ** -0.5)).astype(BF16)

    pt = lax.dot_general(wt_ref[...], h, DN_T, preferred_element_type=F32)
    pt_ref[:2 * ML_INNER, :] = pt[:2 * ML_INNER].astype(BF16)
    cost = cost_ref[...]
    sst = sst_ref[...]
    row = lax.broadcasted_iota(jnp.int32, (2 * ML_DK, 1), 0)
    first_r = (row % (2 * half)) < half
    for blk in range(ML_QK // 128):
        rows = slice(2 * ML_INNER + blk * 128, 2 * ML_INNER + (blk + 1) * 128)
        xb = pt[rows]
        sw = jnp.where(first_r, pltpu.roll(xb, 128 - half, 0), pltpu.roll(xb, half, 0))
        pt_ref[rows, :] = (xb * cost + sw * sst).astype(BF16)


def _ml_inproj(xs, mods, ng, w_in, w_gate, b_gate):
    bsz, ttot, _ = xs.shape
    ngate = 4 * ML_HEADS
    cos, ss = _rope_tables(ttot - TB)
    wk = w_in[:, ML_QK:2 * ML_QK]
    wt = jnp.concatenate([w_in[:, 2 * ML_QK + ML_INNER:], w_in[:, 2 * ML_QK:2 * ML_QK + ML_INNER],
                          w_in[:, :ML_QK]], axis=1).T
    wg = jnp.concatenate([w_gate[0], w_gate[1]], axis=1).astype(BF16).T
    bg = jnp.broadcast_to(b_gate.reshape(ngate, 1).astype(F32), (ngate, 128))
    nt = wt.shape[0]
    tab = pl.BlockSpec((TB, 2 * ML_DK), lambda b, j: (j, 0))
    tabT = pl.BlockSpec((2 * ML_DK, TB), lambda b, j: (0, j))
    return pl.pallas_call(
        _ml_inproj_kernel,
        grid=(bsz, ttot // TB),
        in_specs=[_tok_spec(D), _mod_spec(), _const_spec((4, D)), _const_spec(wk.shape), _const_spec(wt.shape),
                  tab, tab, tabT, tabT, _const_spec((ngate, D)), _const_spec((ngate, 128))],
        out_specs=[_tok_spec(ML_QK), _tokT_spec(nt), _tokT_spec(6 * ML_HEADS), _tok_spec(128)],
        out_shape=[jax.ShapeDtypeStruct((bsz, ttot, ML_QK), BF16),
                   jax.ShapeDtypeStruct((bsz, ttot // TB, nt, TB), BF16),
                   jax.ShapeDtypeStruct((bsz, ttot // TB, 6 * ML_HEADS, TB), F32),
                   jax.ShapeDtypeStruct((bsz, ttot, 128), F32)],
        compiler_params=_cparams(("arbitrary", "arbitrary")),
        name="ml_inproj",
    )(xs, mods, ng, wk, wt, cos, ss, cos.T, ss.T, wg, bg)


def _ml_scan_kernel(kf_ref, vf_ref, qf_ref, grf_ref, gcf_ref, kb_ref, vb_ref, qb_ref, grb_ref, gcb_ref,
                    hf_ref, hb_ref, s_scr, m_scr):
    step = pl.program_id(1)

    @pl.when(step == 0)
    def _():
        s_scr[...] = jnp.zeros_like(s_scr)
        m_scr[...] = jnp.zeros_like(m_scr)

    nh = ML_HEADS
    lane = lax.broadcasted_iota(jnp.int32, (1, 2 * ML_DK), 1)
    r_i = lax.broadcasted_iota(jnp.int32, (TB, TB), 0)
    c_i = lax.broadcasted_iota(jnp.int32, (TB, TB), 1)
    ones_rows = jnp.ones((ML_ONES, TB), BF16)

    dirs = ((kf_ref, vf_ref, qf_ref, grf_ref, gcf_ref, hf_ref, r_i <= c_i, TB - 1),
            (kb_ref, vb_ref, qb_ref, grb_ref, gcb_ref, hb_ref, r_i >= c_i, 0))
    stats = []
    for dirn, (_, _, _, gr_ref, _, _, _, last) in enumerate(dirs):
        u = gr_ref[0:nh, :]
        pm = gr_ref[nh:2 * nh, :]
        b = gr_ref[2 * nh:3 * nh, :]
        m_st = m_scr[dirn][:, 0:1]
        mrow = jnp.maximum(m_st, pm)
        m_last = mrow[:, last:last + 1]
        stats.append(dict(mrow=mrow, inter=jnp.exp(m_st - mrow), em=jnp.exp(-(b + mrow)),
                          wrow=jnp.exp(u - m_last), decay=jnp.exp(m_st - m_last)))
        m_scr[dirn] = jnp.broadcast_to(b[:, last:last + 1] + m_last, (nh, 128))

    def stage_a(i):
        dirn, hd = divmod(i, nh)
        k_ref, v_ref, q_ref = dirs[dirn][:3]
        pair, a = divmod(hd, 2)
        kp = k_ref[:, pair * 128:(pair + 1) * 128]
        sel = (lane >= a * ML_DK) & (lane < (a + 1) * ML_DK)
        km = jnp.where(sel, kp, jnp.zeros_like(kp))
        qt = q_ref[pair * 128:(pair + 1) * 128, :]
        vaug = jnp.concatenate([v_ref[hd * ML_DV:(hd + 1) * ML_DV, :], ones_rows], axis=0)
        st = s_scr[i]
        s_t = jnp.dot(km, qt, preferred_element_type=F32)
        r2 = jnp.dot(st.astype(BF16), qt, preferred_element_type=F32)
        return km, vaug, st, s_t, r2

    def stage_b(i, km, vaug, st, s_t, r2):
        dirn, hd = divmod(i, nh)
        gc_ref, out_ref, tri = dirs[dirn][4:7]
        sd = stats[dirn]
        u_col = gc_ref[:, i:i + 1]
        d_t = jnp.where(tri, jnp.exp(u_col - sd["mrow"][hd:hd + 1, :]), 0.0)
        a_t = (s_t * d_t).astype(BF16)
        res = jnp.dot(vaug, a_t, preferred_element_type=F32) + sd["inter"][hd:hd + 1, :] * r2
        den = jnp.maximum(jnp.abs(res[ML_DV:ML_DV + 1, :]), sd["em"][hd:hd + 1, :])
        out_ref[hd * ML_DV:(hd + 1) * ML_DV, :] = res[:ML_DV] * (1.0 / den)
        vw = (vaug.astype(F32) * sd["wrow"][hd:hd + 1, :]).astype(BF16)
        s_scr[i] = sd["decay"][hd:hd + 1, :] * st + jnp.dot(vw, km, preferred_element_type=F32)

    nxt = stage_a(0)
    for i in range(2 * nh):
        cur = nxt
        if i + 1 < 2 * nh:
            nxt = stage_a(i + 1)
        stage_b(i, *cur)


def _ml_scan(kt, pt, grow, gcol):
    bsz, ttot, _ = kt.shape
    nblk = ttot // TB
    fwd = lambda i: i
    bwd = lambda i: jnp.where(i == 0, 0, nblk - i)
    vrow = ML_INNER // ML_INNER
    qrow = 2 * ML_INNER // ML_QK

    def specs(fn, dirn):
        return [pl.BlockSpec((None, TB, ML_QK), lambda b, i: (b, fn(i), 0)),
                pl.BlockSpec((None, None, ML_INNER, TB), lambda b, i: (b, fn(i), vrow, 0)),
                pl.BlockSpec((None, None, ML_QK, TB), lambda b, i: (b, fn(i), qrow, 0)),
                pl.BlockSpec((None, None, 3 * ML_HEADS, TB), lambda b, i: (b, fn(i), dirn, 0)),
                pl.BlockSpec((None, TB, 128), lambda b, i: (b, fn(i), 0))]

    return pl.pallas_call(
        _ml_scan_kernel,
        grid=(bsz, nblk),
        in_specs=specs(fwd, 0) + specs(bwd, 1),
        out_specs=[pl.BlockSpec((None, None, ML_INNER, TB), lambda b, i: (b, fwd(i), 0, 0)),
                   pl.BlockSpec((None, None, ML_INNER, TB), lambda b, i: (b, bwd(i), 0, 0))],
        out_shape=[jax.ShapeDtypeStruct((bsz, nblk, ML_INNER, TB), F32)] * 2,
        scratch_shapes=[pltpu.VMEM((2 * ML_HEADS, ML_DV + ML_ONES, 2 * ML_DK), F32),
                        pltpu.VMEM((2, ML_HEADS, 128), F32)],
        compiler_params=_cparams(("arbitrary", "arbitrary")),
        name="ml_scan",
    )(kt, pt, pt, grow, gcol, kt, pt, pt, grow, gcol)


def _ml_layer(xs, mods, ng, w_in, w_gate, b_gate, ml_g, w_out):
    kt, pt, grow, gcol = _ml_inproj(xs, mods, ng, w_in, w_gate, b_gate)
    hf, hb = _ml_scan(kt, pt, grow, gcol)
    return ("ml", hf, hb, pt, ml_g, w_out)


def kernel(x, c, ctx, c_ctx, ada_w, ada_b, norm_g, ffn_w1, ffn_w2, gm_w_in, gm_b_in, gm_ln_g, gm_ws, gm_bs, gm_w_out, na_w_qkv, na_b_qkv, na_rpb, na_w_o, na_b_o, ml_w_in, ml_w_gate, ml_b_gate, ml_norm_g, ml_w_out):
    depth = ada_w.shape[0]
    assert ctx.shape[1] == TB and x.shape[1] % TB == 0 and x.shape[0] <= CTX_ROW
    mods_all = _mods_all(c, c_ctx, ada_w, ada_b)
    xs = (ctx, x) if depth > 1 else jnp.concatenate([ctx, x], axis=1)
    for i in range(depth):
        kind, j = i % 3, i // 3
        mods = mods_all[i]
        ng = norm_g[i]
        last = i == depth - 1
        mixer = None
        if kind == 0:
            xs = _gmlp_layer(xs, mods, ng, gm_w_in[j].astype(BF16), gm_b_in[j], gm_ln_g[j],
                             gm_ws[j].astype(BF16), gm_bs[j], gm_w_out[j].astype(BF16), last)
        elif kind == 1:
            mixer = _na_layer(xs, mods, ng, na_w_qkv[j].astype(BF16), na_b_qkv[j], na_rpb[j],
                              na_w_o[j].astype(BF16), na_b_o[j])
        else:
            mixer = _ml_layer(xs, mods, ng, ml_w_in[j].astype(BF16), ml_w_gate[j], ml_b_gate[j],
                              ml_norm_g[j], ml_w_out[j].astype(BF16))
        has_ctx = xs.shape[1] != x.shape[1]
        xs = _ffn_layer(xs, mods, ng, ffn_w1[i].astype(BF16), ffn_w2[i].astype(BF16), has_ctx, has_ctx and last,
                        mixer)
    return xs
```

```python
import functools

import numpy as np
import jax
import jax.numpy as jnp
from jax import lax
from jax.experimental import pallas as pl
from jax.experimental.pallas import tpu as pltpu

F32 = jnp.float32
BF16 = jnp.bfloat16

D = 1024
TB = 256
GRID_W = 64
EPS = 1e-6
CTX_ROW = 8
MOD_ROWS = 16
GM_HALF = 3 * D
GM_CHUNK = 128
GM_GROUPS = 8
GM_GW = GM_HALF // GM_GROUPS
NA_HEADS = 16
NA_DH = 64
NA_KH = 8
NA_KW = 16
NA_RB = TB // GRID_W
NA_KR = 3 * NA_RB
NEG = -1e30
LOG2E = float(np.log2(np.e))
NA_QSCALE = NA_DH ** -0.5 * LOG2E
NA_PAIRS = 4
ML_HEADS = 8
ML_DK = 64
ML_DV = 128
ML_QK = ML_HEADS * ML_DK
ML_INNER = ML_HEADS * ML_DV
ML_ONES = 16
ROPE_BASE = 10000.0

VMEM_LIMIT = 56 * 1024 * 1024

DN_T = (((1,), (1,)), ((), ()))
DN_0 = (((0,), (0,)), ((), ()))


def _cparams(sem, flags=None):
    return pltpu.CompilerParams(dimension_semantics=sem, vmem_limit_bytes=VMEM_LIMIT, flags=flags)


def _rms(x, g):
    ms = jnp.mean(x * x, axis=-1, keepdims=True)
    return x * lax.rsqrt(ms + EPS) * g


def _premod(x, g, shift, scale):
    return _rms(x, g) * (1.0 + scale) + shift


def _const_spec(shape):
    nd = len(shape)
    return pl.BlockSpec(shape, lambda *_: (0,) * nd, pipeline_mode=pl.Buffered(1))


def _tok_spec(width, off=0, col=0):
    return pl.BlockSpec((None, TB, width), lambda b, j: (b, j + off, col))


def _tokT_spec(height):
    return pl.BlockSpec((None, None, height, TB), lambda b, j: (b, j, 0, 0))


def _mod_spec(ctx_first=True):
    if ctx_first:
        return pl.BlockSpec((None, 6, D), lambda b, j: (jnp.where(j == 0, CTX_ROW, b), 0, 0))
    return pl.BlockSpec((None, 6, D), lambda b, j: (b, 0, 0))


def _mod_kernel(c_ref, w_ref, b_ref, o_ref):
    c = c_ref[...]
    a = (c * jax.nn.sigmoid(c)).astype(BF16)
    o_ref[...] = jnp.dot(a, w_ref[...].astype(BF16), preferred_element_type=F32) + b_ref[...]


def _mods_all(c, c_ctx, ada_w, ada_b):
    depth = ada_w.shape[0]
    n = ada_w.shape[2]
    tn = 1536
    cv = jnp.zeros((MOD_ROWS, D), F32).at[:c.shape[0]].set(c).at[CTX_ROW].set(c_ctx)
    out = pl.pallas_call(
        _mod_kernel,
        grid=(depth, n // tn),
        in_specs=[pl.BlockSpec((MOD_ROWS, D), lambda l, k: (0, 0)),
                  pl.BlockSpec((None, D, tn), lambda l, k: (l, 0, k)),
                  pl.BlockSpec((None, 1, tn), lambda l, k: (l, 0, k))],
        out_specs=pl.BlockSpec((None, MOD_ROWS, tn), lambda l, k: (l, 0, k)),
        out_shape=jax.ShapeDtypeStruct((depth, MOD_ROWS, n), F32),
        compiler_params=_cparams(("arbitrary", "arbitrary")),
        name="adaln_mod",
    )(cv, ada_w, ada_b.reshape(depth, 1, n))
    return out.reshape(depth, MOD_ROWS, 6, D)


def _software_pipeline(n_items, stages):
    carry = [None] * n_items
    for t in range(n_items + len(stages) - 1):
        for k, stage in enumerate(stages):
            if 0 <= t - k < n_items:
                carry[t - k] = stage(t - k, carry[t - k])


def _sub_plan(bsz, nblk, off, over_batch):
    if over_batch:
        n = 2 if bsz % 2 == 0 else 1
        return (n, (bsz // n, nblk), lambda s: (lambda b, j: n * b + s), lambda s: (lambda b, j: off + j),
                pl.BlockSpec((n, TB, D), lambda b, j: (b, j, 0)))
    n = 3 if nblk % 3 == 0 else (2 if nblk % 2 == 0 else 1)
    return (n, (bsz, nblk // n), lambda s: (lambda b, j: b), lambda s: (lambda b, j: off + n * j + s),
            pl.BlockSpec((None, n * TB, D), lambda b, j: (b, j, 0)))


def _store_sub(o_ref, s, over_batch, val):
    if over_batch:
        o_ref[s] = val
    else:
        o_ref[s * TB:(s + 1) * TB, :] = val


def _ffn_kernel(*refs, n_sub, over_batch, mixer):
    per = {None: 2, "na": 3, "ml": 5}[mixer]
    subs = [refs[s * per:(s + 1) * per] for s in range(n_sub)]
    consts, o_ref = refs[n_sub * per:-1], refs[-1]
    g = consts[0][...]
    w1_ref, w2_ref = consts[-2:]

    def mixer_proj(s, _):
        if mixer == "na":
            wo_ref, bo_ref = consts[1:3]
            return lax.dot_general(subs[s][2][...], wo_ref[...], DN_0, preferred_element_type=F32) + bo_ref[...]
        hf_ref, hb_ref, og_ref = subs[s][2:5]
        mlg_ref, wo_ref = consts[1:3]
        hs = hf_ref[...] + hb_ref[...]
        parts = []
        for hd in range(ML_HEADS):
            xh = hs[hd * ML_DV:(hd + 1) * ML_DV]
            parts.append(xh * lax.rsqrt(jnp.mean(xh * xh, axis=0, keepdims=True) + EPS))
        mlg = mlg_ref[...]
        hh = jnp.concatenate(parts, axis=0) * jnp.concatenate([mlg, mlg], axis=1)
        yin = (hh * jax.nn.sigmoid(og_ref[...].astype(F32))).astype(BF16)
        return lax.dot_general(yin, wo_ref[...], DN_0, preferred_element_type=F32)

    def up(s, ymix):
        x = subs[s][0][...]
        mod = subs[s][1][...]
        if mixer is not None:
            x = x + mod[2:3] * _rms(ymix, g[1:2])
        h = _premod(x, g[2:3], mod[3:4], mod[4:5]).astype(BF16)
        return x, jnp.dot(h, w1_ref[...], preferred_element_type=F32)

    def down(s, c):
        a = jnp.square(jnp.maximum(c[1], 0.0)).astype(BF16)
        return c[0], jnp.dot(a, w2_ref[...], preferred_element_type=F32)

    def finish(s, c):
        mod = subs[s][1][...]
        _store_sub(o_ref, s, over_batch, c[0] + mod[5:6] * _rms(c[1], g[3:4]))

    _software_pipeline(n_sub, ([] if mixer is None else [mixer_proj]) + [up, down, finish])


def _ffn_layer(xs, mods, ng, w1, w2, has_ctx, drop_ctx, mixer=None):
    bsz, ttot, _ = xs.shape
    off = 1 if drop_ctx else 0
    nblk = ttot // TB - off
    over_batch = mixer is not None and mixer[0] == "ml"
    n_sub, grid, b_of, j_of, out_spec = _sub_plan(bsz, nblk, off, over_batch)
    ctx_first = has_ctx and not drop_ctx

    def mod_spec(s):
        bi, ji = b_of(s), j_of(s)
        row = (lambda b, j: jnp.where(ji(b, j) == 0, CTX_ROW, bi(b, j))) if ctx_first else bi
        return pl.BlockSpec((None, 6, D), lambda b, j: (row(b, j), 0, 0))

    def tok(s):
        bi, ji = b_of(s), j_of(s)
        return pl.BlockSpec((None, TB, D), lambda b, j: (bi(b, j), ji(b, j), 0))

    def tokT(s, height):
        bi, ji = b_of(s), j_of(s)
        return pl.BlockSpec((None, None, height, TB), lambda b, j: (bi(b, j), ji(b, j), 0, 0))

    kind, per_sub, consts, const_specs = None, (), (), []
    if mixer is not None and mixer[0] == "na":
        kind, yt, w_o, b_o = mixer
        per_sub, consts = ((yt, D),), (w_o, b_o.reshape(1, D))
        const_specs = [_const_spec(w_o.shape), _const_spec((1, D))]
    elif mixer is not None:
        kind, hf, hb, pt, ml_g, w_o = mixer
        mlg = jnp.broadcast_to(ml_g.astype(F32)[:, None], (ML_INNER, 128))
        per_sub, consts = ((hf, ML_INNER), (hb, ML_INNER), (pt, ML_INNER)), (mlg, w_o)
        const_specs = [_const_spec((ML_INNER, 128)), _const_spec(w_o.shape)]
    args, specs = [], []
    for s in range(n_sub):
        args += [xs, mods] + [a for a, _ in per_sub]
        specs += [tok(s), mod_spec(s)] + [tokT(s, hgt) for _, hgt in per_sub]
    return pl.pallas_call(
        functools.partial(_ffn_kernel, n_sub=n_sub, over_batch=over_batch, mixer=kind),
        grid=grid,
        in_specs=specs + [_const_spec((4, D))] + const_specs + [_const_spec(w1.shape), _const_spec(w2.shape)],
        out_specs=out_spec,
        out_shape=jax.ShapeDtypeStruct((bsz, nblk * TB, D), F32),
        compiler_params=_cparams(("arbitrary", "arbitrary")),
        name="ffn" if mixer is None else kind + "_ffn",
    )(*args, ng, *consts, w1, w2)


def _gelu_tanh(x):
    return 0.5 * x * (1.0 + jnp.tanh(np.float32(np.sqrt(2.0 / np.pi)) * (x + 0.044715 * (x * x * x))))


def _gmlp_kernel(*refs, n_sub, split_input):
    per = 3 if split_input else 2
    subs = [refs[s * per:(s + 1) * per] for s in range(n_sub)]
    g_ref, win_ref, bin_ref, lng_ref, ws_ref, bst_ref, wout_ref, o_ref, v_scr, t_scr = refs[n_sub * per:]
    g = g_ref[...]

    def v_proj(s, _):
        if split_input:
            x = jnp.where(pl.program_id(1) == 0, subs[s][0][...], subs[s][1][...])
        else:
            x = subs[s][0][...]
        mod = subs[s][-1][...]
        h = _premod(x, g[0:1], mod[0:1], mod[1:2]).astype(BF16)
        return x, h, jnp.dot(h, win_ref[:, GM_HALF:], preferred_element_type=F32) + bin_ref[:, GM_HALF:]

    def u_proj(s, c):
        x, h, zv = c
        v = _gelu_tanh(zv)
        vc = v - jnp.mean(v, axis=-1, keepdims=True)
        v = vc * lax.rsqrt(jnp.mean(vc * vc, axis=-1, keepdims=True) + EPS) * lng_ref[...]
        v_scr[s] = v.astype(BF16)
        return x, jnp.dot(h, win_ref[:, :GM_HALF], preferred_element_type=F32) + bin_ref[:, :GM_HALF]

    def mix(s, c):
        x, zu = c
        u = _gelu_tanh(zu)
        bst = bst_ref[...]
        for n in range(TB // GM_CHUNK):
            rows = slice(n * GM_CHUNK, (n + 1) * GM_CHUNK)
            for gi in range(GM_GROUPS):
                cols = slice(gi * GM_GW, (gi + 1) * GM_GW)
                sp = jnp.dot(ws_ref[gi], v_scr[s, rows, cols], preferred_element_type=F32) + bst[:, gi:gi + 1]
                t_scr[s, rows, cols] = (u[rows, cols] * sp).astype(BF16)
        return x, jnp.dot(t_scr[s], wout_ref[...], preferred_element_type=F32)

    def finish(s, c):
        mod = subs[s][-1][...]
        o_ref[s] = c[0] + mod[2:3] * _rms(c[1], g[1:2])

    _software_pipeline(n_sub, [v_proj, u_proj, mix, finish])


def _gmlp_layer(xs, mods, ng, w_in, b_in, ln_g, ws, bs, w_out, drop_ctx):
    split = isinstance(xs, tuple)
    if split:
        assert not drop_ctx
        bsz, ttot = xs[1].shape[0], xs[0].shape[1] + xs[1].shape[1]
    else:
        bsz, ttot, _ = xs.shape
        xs = (xs,)
    off = 1 if drop_ctx else 0
    nblk = ttot // TB - off
    n_sub, grid, b_of, j_of, out_spec = _sub_plan(bsz, nblk, off, True)
    args, specs = [], []
    for s in range(n_sub):
        bi, ji = b_of(s), j_of(s)
        if split:
            specs += [pl.BlockSpec((None, TB, D), lambda b, j, bi=bi: (bi(b, j), 0, 0)),
                      pl.BlockSpec((None, TB, D), lambda b, j, bi=bi: (bi(b, j), jnp.maximum(j - 1, 0), 0))]
        else:
            specs += [pl.BlockSpec((None, TB, D), lambda b, j, bi=bi, ji=ji: (bi(b, j), ji(b, j), 0))]
        row = (lambda b, j, bi=bi: bi(b, j)) if drop_ctx else (lambda b, j, bi=bi: jnp.where(j == 0, CTX_ROW, bi(b, j)))
        specs += [pl.BlockSpec((None, 6, D), lambda b, j, row=row: (row(b, j), 0, 0))]
        args += list(xs) + [mods]
    return pl.pallas_call(
        functools.partial(_gmlp_kernel, n_sub=n_sub, split_input=split),
        grid=grid,
        in_specs=specs + [_const_spec((4, D)),
                          _const_spec(w_in.shape), _const_spec((1, 2 * GM_HALF)), _const_spec((1, GM_HALF)),
                          _const_spec(ws.shape), _const_spec((GM_CHUNK, GM_GROUPS)), _const_spec(w_out.shape)],
        out_specs=out_spec,
        out_shape=jax.ShapeDtypeStruct((bsz, nblk * TB, D), F32),
        scratch_shapes=[pltpu.VMEM((n_sub, TB, GM_HALF), BF16), pltpu.VMEM((n_sub, TB, GM_HALF), BF16)],
        compiler_params=_cparams(("arbitrary", "arbitrary")),
        name="gmlp",
    )(*args, ng, w_in, b_in.reshape(1, -1), ln_g.reshape(1, -1), ws, bs.T, w_out)


def _na_inproj_kernel(x_ref, mod_ref, g_ref, wk_ref, bk_ref, wt_ref, bt_ref, k_ref, pt_ref):
    mod = mod_ref[...]
    g = g_ref[...]
    h = _premod(x_ref[...], g[0:1], mod[0:1], mod[1:2]).astype(BF16)
    k_ref[...] = (jnp.dot(h, wk_ref[...], preferred_element_type=F32) + bk_ref[...]).astype(BF16)
    bt = bt_ref[...]
    pt = lax.dot_general(wt_ref[...], h, DN_T, preferred_element_type=F32) + jnp.concatenate([bt, bt], axis=1)
    pt_ref[:D, :] = (pt[:D] * NA_QSCALE).astype(BF16)
    pt_ref[D:, :] = pt[D:].astype(BF16)


def _na_inproj(xs, mods, ng, w_qkv, b_qkv):
    bsz, ttot, _ = xs.shape
    wk = w_qkv[:, D:2 * D]
    wt = jnp.concatenate([w_qkv[:, :D], w_qkv[:, 2 * D:]], axis=1).T
    bt = jnp.concatenate([b_qkv[:D], b_qkv[2 * D:]]).astype(F32)
    bt = jnp.broadcast_to(bt[:, None], (2 * D, 128))
    return pl.pallas_call(
        _na_inproj_kernel,
        grid=(bsz, ttot // TB),
        in_specs=[_tok_spec(D), _mod_spec(), _const_spec((4, D)), _const_spec((D, D)), _const_spec((1, D)),
                  _const_spec((2 * D, D)), _const_spec((2 * D, 128))],
        out_specs=[pl.BlockSpec((None, None, TB, D), lambda b, j: (b, j, 0, 0)), _tokT_spec(2 * D)],
        out_shape=[jax.ShapeDtypeStruct((bsz, ttot // TB, TB, D), BF16),
                   jax.ShapeDtypeStruct((bsz, ttot // TB, 2 * D, TB), BF16)],
        compiler_params=_cparams(("arbitrary", "arbitrary")),
        name="na_inproj",
    )(xs, mods, ng, wk, b_qkv[D:2 * D].reshape(1, D), wt, bt)


def _na_bias_table(rpb, rows):
    assert rows >= 4 * NA_RB and rows % NA_RB == 0
    rr = np.arange(NA_RB)
    ib = np.arange(NA_KR)
    qc = np.arange(GRID_W)
    kc = np.arange(GRID_W)
    ws = np.clip(qc - NA_KW // 2, 0, GRID_W - NA_KW)
    valid_col = (kc[None, :] >= ws[:, None]) & (kc[None, :] < ws[:, None] + NA_KW)
    dcol = np.clip(kc[None, :] - qc[:, None], -(NA_KW - 1), NA_KW - 1) + NA_KW - 1
    tiles = jnp.where(jnp.asarray(valid_col)[None, None], rpb[:, :, dcol].astype(F32) * LOG2E, NEG)
    drows, valids = [], []
    for off, rs_minus_r in ((rr, np.full(NA_RB, -(NA_KH // 2))),
                            (np.zeros(NA_RB, np.int64), -rr),
                            (np.full(NA_RB, NA_RB), -(NA_KH // 2) - rr)):
        i = ib[None, :] - off[:, None]
        valids.append((i >= 0) & (i < NA_KH))
        drows.append(np.clip(rs_minus_r[:, None] + i + NA_KH - 1, 0, 2 * NA_KH - 2))
    drow = np.stack(drows)
    valid_row = np.stack(valids)
    t = tiles[:, drow]
    t = jnp.where(jnp.asarray(valid_row)[None, :, :, :, None, None], t, NEG)
    t = t.reshape(NA_HEADS // 2, 2, 3, NA_RB, NA_KR, GRID_W, GRID_W)
    t = jnp.transpose(t, (2, 0, 4, 6, 1, 3, 5))
    return t.reshape(3, NA_HEADS // 2, NA_KR * GRID_W, 2 * TB)


def _na_kernel(q_ref, kw_ref, kx_ref, vw_ref, vx_ref, bias_ref, o_ref):
    j = pl.program_id(1)
    row = lax.broadcasted_iota(jnp.int32, (2 * NA_DH, 1), 0)

    def heads(nwin):
        def scores(pair):
            blk = slice(pair * 2 * NA_DH, (pair + 1) * 2 * NA_DH)
            q = q_ref[blk, :]
            zq = jnp.zeros_like(q)
            q2 = jnp.concatenate([jnp.where(row < NA_DH, q, zq), jnp.where(row < NA_DH, zq, q)], axis=1)
            kcat = jnp.concatenate([kw_ref[0, i, :, blk] for i in range(nwin)] + [kx_ref[:, blk]], axis=0)
            return jnp.dot(kcat, q2, preferred_element_type=F32)

        def finish(pair, s):
            blk = slice(pair * 2 * NA_DH, (pair + 1) * 2 * NA_DH)
            ss = [s[i * TB:(i + 1) * TB] + bias_ref[pair, i * TB:(i + 1) * TB, :] if i < nwin
                  else s[i * TB:(i + 1) * TB] for i in range(nwin + 1)]
            m = functools.reduce(jnp.maximum, [jnp.max(s, axis=0, keepdims=True) for s in ss])
            ps = [jnp.exp2(s - m) for s in ss]
            l = functools.reduce(jnp.add, [jnp.sum(p, axis=0, keepdims=True) for p in ps])
            vcat = jnp.concatenate([vw_ref[0, i, blk, :] for i in range(nwin)] + [vx_ref[blk, :]], axis=1)
            pcat = jnp.concatenate([p.astype(BF16) for p in ps], axis=0)
            o = jnp.dot(vcat, pcat, preferred_element_type=F32) * (1.0 / l)
            o_ref[blk, :] = jnp.where(row < NA_DH, o[:, :TB], o[:, TB:]).astype(BF16)

        s_next = scores(0)
        for pair in range(NA_PAIRS):
            s_cur = s_next
            if pair + 1 < NA_PAIRS:
                s_next = scores(pair + 1)
            finish(pair, s_cur)

    @pl.when(j == 0)
    def _():
        heads(0)

    @pl.when(j > 0)
    def _():
        heads(3)


def _na_attention(kt, pt, bias):
    bsz, nblk = kt.shape[:2]
    hp = NA_HEADS // 2 // NA_PAIRS
    wb = 2 * NA_DH * NA_PAIRS
    win = lambda j: jnp.clip(j - 1, 1, nblk - 3)
    case = lambda j: jnp.where(j == 1, 1, jnp.where(j == nblk - 1, 2, 0))
    qo_spec = pl.BlockSpec((None, None, wb, TB), lambda h, j, b: (b, j, h, 0))
    return pl.pallas_call(
        _na_kernel,
        grid=(hp, nblk, bsz),
        in_specs=[qo_spec,
                  pl.BlockSpec((pl.Element(1), pl.Element(3), pl.Element(TB), pl.Element(wb)),
                               lambda h, j, b: (b, win(j), 0, h * wb)),
                  pl.BlockSpec((None, None, TB, wb), lambda h, j, b: (b, 0, 0, h)),
                  pl.BlockSpec((pl.Element(1), pl.Element(3), pl.Element(wb), pl.Element(TB)),
                               lambda h, j, b: (b, win(j), (hp + h) * wb, 0)),
                  pl.BlockSpec((None, None, wb, TB), lambda h, j, b: (b, 0, hp + h, 0)),
                  pl.BlockSpec((None, NA_PAIRS, NA_KR * GRID_W, 2 * TB), lambda h, j, b: (case(j), h, 0, 0))],
        out_specs=qo_spec,
        out_shape=jax.ShapeDtypeStruct((bsz, nblk, D, TB), BF16),
        compiler_params=_cparams(("arbitrary", "arbitrary", "arbitrary")),
        name="na_attention",
    )(pt, kt, kt, pt, pt, bias)


def _na_layer(xs, mods, ng, w_qkv, b_qkv, rpb, w_o, b_o):
    rows = (xs.shape[1] - TB) // GRID_W
    kt, pt = _na_inproj(xs, mods, ng, w_qkv, b_qkv)
    att = _na_attention(kt, pt, _na_bias_table(rpb, rows))
    return ("na", att, w_o, b_o)


def _rope_tables(t_lat):
    pos = jnp.arange(t_lat)
    row = (pos // GRID_W).astype(F32)
    col = (pos % GRID_W).astype(F32)
    d_axis = ML_DK // 2
    inv = ROPE_BASE ** (-jnp.arange(0, d_axis, 2, dtype=F32) / d_axis)
    lane = np.arange(2 * ML_DK)
    d = lane % ML_DK
    use_col = (d // d_axis) == 1
    jdx = d % (d_axis // 2)
    first = (d % d_axis) < d_axis // 2
    ang_r = row[:, None] * inv[None, :]
    ang_c = col[:, None] * inv[None, :]
    ang = jnp.where(jnp.asarray(use_col)[None, :], ang_c[:, jdx], ang_r[:, jdx])
    cos = jnp.cos(ang)
    sin = jnp.sin(ang)
    ss = jnp.where(jnp.asarray(first)[None, :], -sin, sin)
    cos = jnp.concatenate([jnp.ones((TB, 2 * ML_DK), F32), cos], axis=0)
    ss = jnp.concatenate([jnp.zeros((TB, 2 * ML_DK), F32), ss], axis=0)
    return cos, ss


def _log_sigmoid(x):
    return jnp.minimum(x, 0.0) - jnp.log(1.0 + jnp.exp(-jnp.abs(x)))


def _lane_scan(x, op, ident, reverse):
    n = x.shape[1]
    lane = lax.broadcasted_iota(jnp.int32, (1, n), 1)
    d = 1
    while d < n:
        if reverse:
            sh = jnp.where(lane < n - d, pltpu.roll(x, n - d, 1), ident)
        else:
            sh = jnp.where(lane >= d, pltpu.roll(x, d, 1), ident)
        x = op(x, sh)
        d *= 2
    return x


def _ml_inproj_kernel(x_ref, mod_ref, g_ref, wk_ref, wt_ref, cos_ref, ss_ref, cost_ref, sst_ref,
                      wg_ref, bg_ref, k_ref, pt_ref, gr_ref, gc_ref):
    mod = mod_ref[...]
    g = g_ref[...]
    h = _premod(x_ref[...], g[0:1], mod[0:1], mod[1:2]).astype(BF16)
    half = ML_DK // 4

    nh = ML_HEADS
    bg = bg_ref[...]
    gr = lax.dot_general(wg_ref[...], h, DN_T, preferred_element_type=F32) + jnp.concatenate([bg, bg], axis=1)
    outs = []
    for dirn in range(2):
        ig = gr[2 * dirn * nh:(2 * dirn + 1) * nh]
        lf = _log_sigmoid(gr[(2 * dirn + 1) * nh:(2 * dirn + 2) * nh])
        b = _lane_scan(lf, jnp.add, 0.0, dirn == 1)
        u = ig - b
        outs += [u, _lane_scan(u, jnp.maximum, -jnp.inf, dirn == 1), b]
    gr_ref[...] = jnp.concatenate(outs, axis=0)
    ucols = jnp.concatenate([outs[0], outs[3], jnp.zeros((128 - 2 * nh, TB), F32)], axis=0)
    gc_ref[...] = ucols.T

    k = jnp.dot(h, wk_ref[...], preferred_element_type=F32)
    cos = cos_ref[...]
    ss = ss_ref[...]
    lane = lax.broadcasted_iota(jnp.int32, (1, 2 * ML_DK), 1)
    first_l = (lane % (2 * half)) < half
    for blk in range(ML_QK // 128):
        cols = slice(blk * 128, (blk + 1) * 128)
        xb = k[:, cols]
        sw = jnp.where(first_l, pltpu.roll(xb, 128 - half, 1), pltpu.roll(xb, half, 1))
        k_ref[:, cols] = ((xb * cos + sw * ss) * (ML_DK ** -0.5)).astype(BF16)

    pt = lax.dot_general(wt_ref[...], h, DN_T, preferred_element_type=F32)
    pt_ref[:2 * ML_INNER, :] = pt[:2 * ML_INNER].astype(BF16)
    cost = cost_ref[...]
    sst = sst_ref[...]
    row = lax.broadcasted_iota(jnp.int32, (2 * ML_DK, 1), 0)
    first_r = (row % (2 * half)) < half
    for blk in range(ML_QK // 128):
        rows = slice(2 * ML_INNER + blk * 128, 2 * ML_INNER + (blk + 1) * 128)
        xb = pt[rows]
        sw = jnp.where(first_r, pltpu.roll(xb, 128 - half, 0), pltpu.roll(xb, half, 0))
        pt_ref[rows, :] = (xb * cost + sw * sst).astype(BF16)


def _ml_inproj(xs, mods, ng, w_in, w_gate, b_gate):
    bsz, ttot, _ = xs.shape
    ngate = 4 * ML_HEADS
    cos, ss = _rope_tables(ttot - TB)
    wk = w_in[:, ML_QK:2 * ML_QK]
    wt = jnp.concatenate([w_in[:, 2 * ML_QK + ML_INNER:], w_in[:, 2 * ML_QK:2 * ML_QK + ML_INNER],
                          w_in[:, :ML_QK]], axis=1).T
    wg = jnp.concatenate([w_gate[0], w_gate[1]], axis=1).astype(BF16).T
    bg = jnp.broadcast_to(b_gate.reshape(ngate, 1).astype(F32), (ngate, 128))
    nt = wt.shape[0]
    tab = pl.BlockSpec((TB, 2 * ML_DK), lambda b, j: (j, 0))
    tabT = pl.BlockSpec((2 * ML_DK, TB), lambda b, j: (0, j))
    return pl.pallas_call(
        _ml_inproj_kernel,
        grid=(bsz, ttot // TB),
        in_specs=[_tok_spec(D), _mod_spec(), _const_spec((4, D)), _const_spec(wk.shape), _const_spec(wt.shape),
                  tab, tab, tabT, tabT, _const_spec((ngate, D)), _const_spec((ngate, 128))],
        out_specs=[_tok_spec(ML_QK), _tokT_spec(nt), _tokT_spec(6 * ML_HEADS), _tok_spec(128)],
        out_shape=[jax.ShapeDtypeStruct((bsz, ttot, ML_QK), BF16),
                   jax.ShapeDtypeStruct((bsz, ttot // TB, nt, TB), BF16),
                   jax.ShapeDtypeStruct((bsz, ttot // TB, 6 * ML_HEADS, TB), F32),
                   jax.ShapeDtypeStruct((bsz, ttot, 128), F32)],
        compiler_params=_cparams(("arbitrary", "arbitrary")),
        name="ml_inproj",
    )(xs, mods, ng, wk, wt, cos, ss, cos.T, ss.T, wg, bg)


def _ml_scan_kernel(kf_ref, vf_ref, qf_ref, grf_ref, gcf_ref, kb_ref, vb_ref, qb_ref, grb_ref, gcb_ref,
                    hf_ref, hb_ref, s_scr, m_scr):
    step = pl.program_id(1)

    @pl.when(step == 0)
    def _():
        s_scr[...] = jnp.zeros_like(s_scr)
        m_scr[...] = jnp.zeros_like(m_scr)

    nh = ML_HEADS
    lane = lax.broadcasted_iota(jnp.int32, (1, 2 * ML_DK), 1)
    r_i = lax.broadcasted_iota(jnp.int32, (TB, TB), 0)
    c_i = lax.broadcasted_iota(jnp.int32, (TB, TB), 1)
    ones_rows = jnp.ones((ML_ONES, TB), BF16)

    dirs = ((kf_ref, vf_ref, qf_ref, grf_ref, gcf_ref, hf_ref, r_i <= c_i, TB - 1),
            (kb_ref, vb_ref, qb_ref, grb_ref, gcb_ref, hb_ref, r_i >= c_i, 0))
    stats = []
    for dirn, (_, _, _, gr_ref, _, _, _, last) in enumerate(dirs):
        u = gr_ref[0:nh, :]
        pm = gr_ref[nh:2 * nh, :]
        b = gr_ref[2 * nh:3 * nh, :]
        m_st = m_scr[dirn][:, 0:1]
        mrow = jnp.maximum(m_st, pm)
        m_last = mrow[:, last:last + 1]
        stats.append(dict(mrow=mrow, inter=jnp.exp(m_st - mrow), em=jnp.exp(-(b + mrow)),
                          wrow=jnp.exp(u - m_last), decay=jnp.exp(m_st - m_last)))
        m_scr[dirn] = jnp.broadcast_to(b[:, last:last + 1] + m_last, (nh, 128))

    def stage_a(i):
        dirn, hd = divmod(i, nh)
        k_ref, v_ref, q_ref = dirs[dirn][:3]
        pair, a = divmod(hd, 2)
        kp = k_ref[:, pair * 128:(pair + 1) * 128]
        sel = (lane >= a * ML_DK) & (lane < (a + 1) * ML_DK)
        km = jnp.where(sel, kp, jnp.zeros_like(kp))
        qt = q_ref[pair * 128:(pair + 1) * 128, :]
        vaug = jnp.concatenate([v_ref[hd * ML_DV:(hd + 1) * ML_DV, :], ones_rows], axis=0)
        st = s_scr[i]
        s_t = jnp.dot(km, qt, preferred_element_type=F32)
        r2 = jnp.dot(st.astype(BF16), qt, preferred_element_type=F32)
        return km, vaug, st, s_t, r2

    def stage_b(i, km, vaug, st, s_t, r2):
        dirn, hd = divmod(i, nh)
        gc_ref, out_ref, tri = dirs[dirn][4:7]
        sd = stats[dirn]
        u_col = gc_ref[:, i:i + 1]
        d_t = jnp.where(tri, jnp.exp(u_col - sd["mrow"][hd:hd + 1, :]), 0.0)
        a_t = (s_t * d_t).astype(BF16)
        res = jnp.dot(vaug, a_t, preferred_element_type=F32) + sd["inter"][hd:hd + 1, :] * r2
        den = jnp.maximum(jnp.abs(res[ML_DV:ML_DV + 1, :]), sd["em"][hd:hd + 1, :])
        out_ref[hd * ML_DV:(hd + 1) * ML_DV, :] = res[:ML_DV] * (1.0 / den)
        vw = (vaug.astype(F32) * sd["wrow"][hd:hd + 1, :]).astype(BF16)
        s_scr[i] = sd["decay"][hd:hd + 1, :] * st + jnp.dot(vw, km, preferred_element_type=F32)

    nxt = stage_a(0)
    for i in range(2 * nh):
        cur = nxt
        if i + 1 < 2 * nh:
            nxt = stage_a(i + 1)
        stage_b(i, *cur)


def _ml_scan(kt, pt, grow, gcol):
    bsz, ttot, _ = kt.shape
    nblk = ttot // TB
    fwd = lambda i: i
    bwd = lambda i: jnp.where(i == 0, 0, nblk - i)
    vrow = ML_INNER // ML_INNER
    qrow = 2 * ML_INNER // ML_QK

    def specs(fn, dirn):
        return [pl.BlockSpec((None, TB, ML_QK), lambda b, i: (b, fn(i), 0)),
                pl.BlockSpec((None, None, ML_INNER, TB), lambda b, i: (b, fn(i), vrow, 0)),
                pl.BlockSpec((None, None, ML_QK, TB), lambda b, i: (b, fn(i), qrow, 0)),
                pl.BlockSpec((None, None, 3 * ML_HEADS, TB), lambda b, i: (b, fn(i), dirn, 0)),
                pl.BlockSpec((None, TB, 128), lambda b, i: (b, fn(i), 0))]

    return pl.pallas_call(
        _ml_scan_kernel,
        grid=(bsz, nblk),
        in_specs=specs(fwd, 0) + specs(bwd, 1),
        out_specs=[pl.BlockSpec((None, None, ML_INNER, TB), lambda b, i: (b, fwd(i), 0, 0)),
                   pl.BlockSpec((None, None, ML_INNER, TB), lambda b, i: (b, bwd(i), 0, 0))],
        out_shape=[jax.ShapeDtypeStruct((bsz, nblk, ML_INNER, TB), F32)] * 2,
        scratch_shapes=[pltpu.VMEM((2 * ML_HEADS, ML_DV + ML_ONES, 2 * ML_DK), F32),
                        pltpu.VMEM((2, ML_HEADS, 128), F32)],
        compiler_params=_cparams(("arbitrary", "arbitrary")),
        name="ml_scan",
    )(kt, pt, pt, grow, gcol, kt, pt, pt, grow, gcol)


def _ml_layer(xs, mods, ng, w_in, w_gate, b_gate, ml_g, w_out):
    kt, pt, grow, gcol = _ml_inproj(xs, mods, ng, w_in, w_gate, b_gate)
    hf, hb = _ml_scan(kt, pt, grow, gcol)
    return ("ml", hf, hb, pt, ml_g, w_out)


def kernel(x, c, ctx, c_ctx, ada_w, ada_b, norm_g, ffn_w1, ffn_w2, gm_w_in, gm_b_in, gm_ln_g, gm_ws, gm_bs, gm_w_out, na_w_qkv, na_b_qkv, na_rpb, na_w_o, na_b_o, ml_w_in, ml_w_gate, ml_b_gate, ml_norm_g, ml_w_out):
    depth = ada_w.shape[0]
    assert ctx.shape[1] == TB and x.shape[1] % TB == 0 and x.shape[0] <= CTX_ROW
    mods_all = _mods_all(c, c_ctx, ada_w, ada_b)
    xs = (ctx, x) if depth > 1 else jnp.concatenate([ctx, x], axis=1)
    for i in range(depth):
        kind, j = i % 3, i // 3
        mods = mods_all[i]
        ng = norm_g[i]
        last = i == depth - 1
        mixer = None
        if kind == 0:
            xs = _gmlp_layer(xs, mods, ng, gm_w_in[j].astype(BF16), gm_b_in[j], gm_ln_g[j],
                             gm_ws[j].astype(BF16), gm_bs[j], gm_w_out[j].astype(BF16), last)
        elif kind == 1:
            mixer = _na_layer(xs, mods, ng, na_w_qkv[j].astype(BF16), na_b_qkv[j], na_rpb[j],
                              na_w_o[j].astype(BF16), na_b_o[j])
        else:
            mixer = _ml_layer(xs, mods, ng, ml_w_in[j].astype(BF16), ml_w_gate[j], ml_b_gate[j],
                              ml_norm_g[j], ml_w_out[j].astype(BF16))
        has_ctx = xs.shape[1] != x.shape[1]
        xs = _ffn_layer(xs, mods, ng, ffn_w1[i].astype(BF16), ffn_w2[i].astype(BF16), has_ctx, has_ctx and last,
                        mixer)
    return xs
```
